```python
import math
import jax, jax.numpy as jnp
from jax import lax
import numpy as np


D_MODEL = 2048
BATCH = 8
SEQ = 2048
DEPTH = 2

GRID_W = 64
CTX_LEN = 256
MIX_WIDTH = D_MODEL
ATT_WIDTH = MIX_WIDTH // 2
ATT_V_DIM = 128
ATT_QK_DIM = 64
ATT_HEADS = ATT_WIDTH // ATT_V_DIM
MLSTM_WIDTH = MIX_WIDTH - ATT_WIDTH
MLSTM_V_DIM = 256
MLSTM_QK_DIM = 128
MLSTM_HEADS = MLSTM_WIDTH // MLSTM_V_DIM
MLSTM_CHUNK = 64
CONV_WIDTH = 3
GATE_SOFTCAP = 15.0
ROPE_BASE = 10000.0
Q_BLOCK = 128
EPS = 1e-6

IN_SPLITS = (
    ATT_HEADS * 2 * ATT_QK_DIM,
    ATT_HEADS * 2 * ATT_QK_DIM,
    ATT_WIDTH,
    ATT_WIDTH,
    MLSTM_HEADS * MLSTM_QK_DIM,
    MLSTM_HEADS * MLSTM_QK_DIM,
    MLSTM_WIDTH,
    MLSTM_WIDTH,
    MLSTM_WIDTH,
    4 * MLSTM_HEADS,
)
IN_COLS = sum(IN_SPLITS)

kernel_name = 'hybrid_diffattn_mlstm_block'


def rms_norm(x, w):
    xf = x.astype(jnp.float32)
    y = xf * lax.rsqrt(jnp.mean(xf * xf, axis=-1, keepdims=True) + EPS)
    return (y * w.astype(jnp.float32)).astype(x.dtype)


def split_cols(p):
    idx = np.cumsum(IN_SPLITS)[:-1].tolist()
    return jnp.split(p, idx, axis=-1)


def axial_rope_tables(n_tokens):
    rows = n_tokens // GRID_W
    row = jnp.repeat(jnp.arange(rows), GRID_W).astype(jnp.float32)
    col = jnp.tile(jnp.arange(GRID_W), rows).astype(jnp.float32)
    n_freq = ATT_QK_DIM // 4
    inv = ROPE_BASE ** (-jnp.arange(n_freq, dtype=jnp.float32) / n_freq)
    ang = jnp.concatenate([row[:, None] * inv, col[:, None] * inv], axis=-1)
    return jnp.cos(ang), jnp.sin(ang)


def apply_rope(x, cos, sin):
    xf = x.astype(jnp.float32)
    x1, x2 = xf[..., 0::2], xf[..., 1::2]
    c = cos[:, None, None, :]
    s = sin[:, None, None, :]
    out = jnp.stack([x1 * c - x2 * s, x1 * s + x2 * c], axis=-1).reshape(x.shape)
    return out.astype(x.dtype)


def diff_softmax_mix(q, k, v, lam):
    s = jnp.einsum('bqhcd,bkhcd->bhcqk', q, k).astype(jnp.float32) * (ATT_QK_DIM ** -0.5)
    p = jax.nn.softmax(s, axis=-1)
    a = p[:, :, 0] - lam * p[:, :, 1]
    return jnp.einsum('bhqk,bkhe->bqhe', a.astype(v.dtype), v)


def diff_attention_branch(lat, ctx, lam_q1, lam_k1, lam_q2, lam_k2, subln_w, layer_idx, need_ctx_out):
    q, k, v, g = lat
    qc, kc, vc, gc = ctx
    B, T = q.shape[:2]
    f32 = jnp.float32
    lam_init = 0.8 - 0.6 * math.exp(-0.3 * layer_idx)
    lam = (jnp.exp(jnp.sum(lam_q1.astype(f32) * lam_k1.astype(f32)))
           - jnp.exp(jnp.sum(lam_q2.astype(f32) * lam_k2.astype(f32))) + lam_init)
    qk_heads = lambda t: t.reshape(t.shape[0], t.shape[1], ATT_HEADS, 2, ATT_QK_DIM)
    v_heads = lambda t: t.reshape(t.shape[0], t.shape[1], ATT_HEADS, ATT_V_DIM)
    cos, sin = axial_rope_tables(T)
    q = apply_rope(qk_heads(q), cos, sin)
    k = apply_rope(qk_heads(k), cos, sin)
    kc, vc = qk_heads(kc), v_heads(vc)
    k_all = jnp.concatenate([kc, k], axis=1)
    v_all = jnp.concatenate([vc, v_heads(v)], axis=1)
    n_blocks = T // Q_BLOCK
    q_blocks = jnp.swapaxes(q.reshape(B, n_blocks, Q_BLOCK, ATT_HEADS, 2, ATT_QK_DIM), 0, 1)
    o = lax.map(lambda qb: diff_softmax_mix(qb, k_all, v_all, lam), q_blocks)
    o = jnp.swapaxes(o, 0, 1).reshape(B, T, ATT_HEADS, ATT_V_DIM)

    def finish(o, gate):
        o = rms_norm(o, subln_w) * (1.0 - lam_init)
        return o.reshape(o.shape[0], o.shape[1], ATT_WIDTH) * jax.nn.silu(gate)

    y = finish(o, g)
    yc = finish(diff_softmax_mix(qk_heads(qc), kc, vc, lam), gc) if need_ctx_out else None
    return y, yc


def short_conv(x, w, b):
    pad = CONV_WIDTH // 2
    T = x.shape[1]
    xp = jnp.pad(x, ((0, 0), (pad, pad), (0, 0)))
    y = b
    for j in range(CONV_WIDTH):
        y = y + w[j] * xp[:, j:j + T]
    return y


def mlstm_chunkwise(q, k, v, ig, lf, state):
    B, H, T, _ = q.shape
    dv = v.shape[-1]
    nc = T // MLSTM_CHUNK

    def chunks(t):
        return jnp.moveaxis(t.reshape(B, H, nc, MLSTM_CHUNK, *t.shape[3:]), 2, 0)

    tril = jnp.tril(jnp.ones((MLSTM_CHUNK, MLSTM_CHUNK), dtype=bool))

    def step(carry, xs):
        C, n, m = carry
        qc, kc, vc, ic, fc = xs
        b = jnp.cumsum(fc, axis=-1)
        logw = jnp.where(tril, b[..., :, None] - b[..., None, :] + ic[..., None, :], -jnp.inf)
        inter = b + m[..., None]
        m_row = jnp.maximum(inter, jnp.max(logw, axis=-1))
        w = jnp.exp(logw - m_row[..., None])
        s = jnp.einsum('bhld,bhsd->bhls', qc, kc) * w
        sc = jnp.exp(inter - m_row)
        numer = sc[..., None] * jnp.einsum('bhld,bhde->bhle', qc, C) + jnp.einsum('bhls,bhse->bhle', s, vc)
        denom = sc * jnp.einsum('bhld,bhd->bhl', qc, n) + jnp.sum(s, axis=-1)
        h = numer / jnp.maximum(jnp.abs(denom), jnp.exp(-m_row))[..., None]
        b_last = b[..., -1]
        g = b_last[..., None] - b + ic
        m_new = jnp.maximum(b_last + m, jnp.max(g, axis=-1))
        wk = jnp.exp(g - m_new[..., None])
        decay = jnp.exp(b_last + m - m_new)
        C_new = decay[..., None, None] * C + jnp.einsum('bhsd,bhse->bhde', wk[..., None] * kc, vc)
        n_new = decay[..., None] * n + jnp.einsum('bhs,bhsd->bhd', wk, kc)
        return (C_new, n_new, m_new), h

    state, h = lax.scan(step, state, (chunks(q), chunks(k), chunks(v), chunks(ig), chunks(lf)))
    h = jnp.moveaxis(h, 0, 2).reshape(B, H, T, dv)
    return h, state


def mlstm_branch(lat, ctx, conv_w, conv_b, i_bias, f_bias, head_norm_w, need_ctx_out):
    f32 = jnp.float32

    def prep(q, k, v, gates):
        qk = jax.nn.silu(short_conv(jnp.concatenate([q, k], axis=-1), conv_w, conv_b))
        q, k = jnp.split(qk, 2, axis=-1)
        B, T = q.shape[:2]
        heads = lambda t, d: t.reshape(B, T, MLSTM_HEADS, d).transpose(0, 2, 1, 3).astype(f32)
        q = heads(q, MLSTM_QK_DIM)
        k = heads(k, MLSTM_QK_DIM) * (MLSTM_QK_DIM ** -0.5)
        v = heads(v, MLSTM_V_DIM)
        gp = gates.astype(f32).reshape(B, T, 2, 2, MLSTM_HEADS)
        cap = lambda t: GATE_SOFTCAP * jnp.tanh(t / GATE_SOFTCAP)
        ig = cap(gp[:, :, 0] + i_bias.astype(f32))
        lf = jax.nn.log_sigmoid(cap(gp[:, :, 1] + f_bias.astype(f32)))
        return q, k, v, ig.transpose(2, 0, 3, 1), lf.transpose(2, 0, 3, 1)

    ql, kl, vl, igl, lfl = prep(lat[0], lat[1], lat[2], lat[5])
    qc, kc, vc, igc, lfc = prep(ctx[0], ctx[1], ctx[2], ctx[5])
    B = ql.shape[0]
    h_lat, h_ctx = [], []
    for d in range(2):
        flip = (lambda t: jnp.flip(t, axis=2)) if d == 1 else (lambda t: t)
        state0 = (jnp.zeros((B, MLSTM_HEADS, MLSTM_QK_DIM, MLSTM_V_DIM), f32),
                  jnp.zeros((B, MLSTM_HEADS, MLSTM_QK_DIM), f32),
                  jnp.zeros((B, MLSTM_HEADS), f32))
        hc, state = mlstm_chunkwise(flip(qc), flip(kc), flip(vc), flip(igc[d]), flip(lfc[d]), state0)
        hl, _ = mlstm_chunkwise(flip(ql), flip(kl), flip(vl), flip(igl[d]), flip(lfl[d]), state)
        h_lat.append(flip(hl))
        h_ctx.append(flip(hc))

    def finish(h, o, g):
        Bh, H, T, dv = h.shape
        h = rms_norm(h.transpose(0, 2, 1, 3), head_norm_w.reshape(H, dv)).reshape(Bh, T, H * dv)
        return h.astype(o.dtype) * jax.nn.sigmoid(o) * jax.nn.silu(g)

    y = finish(h_lat[0] + h_lat[1], lat[3], lat[4])
    yc = finish(h_ctx[0] + h_ctx[1], ctx[3], ctx[4]) if need_ctx_out else None
    return y, yc


def hybrid_layer(x, xc, mod, mod_c, w_in, w_out, norm_pre, norm_post,
                 lam_q1, lam_k1, lam_q2, lam_k2, attn_subln,
                 conv_w, conv_b, i_bias, f_bias, mlstm_norm, layer_idx, need_ctx_out):
    shift, scale, gate = jnp.split(mod, 3, axis=-1)
    shift_c, scale_c, gate_c = jnp.split(mod_c, 3, axis=-1)
    h = rms_norm(x, norm_pre) * (1.0 + scale[:, None]) + shift[:, None]
    hc = rms_norm(xc, norm_pre) * (1.0 + scale_c) + shift_c
    pl = split_cols(h @ w_in)
    pc = split_cols(hc @ w_in)
    ya, yac = diff_attention_branch(pl[:4], pc[:4], lam_q1, lam_k1, lam_q2, lam_k2,
                                    attn_subln, layer_idx, need_ctx_out)
    ym, ymc = mlstm_branch(pl[4:], pc[4:], conv_w, conv_b, i_bias, f_bias, mlstm_norm, need_ctx_out)
    y = jnp.concatenate([ya, ym], axis=-1) @ w_out
    x = x + gate[:, None] * rms_norm(y, norm_post)
    if need_ctx_out:
        yc = jnp.concatenate([yac, ymc], axis=-1) @ w_out
        xc = xc + gate_c * rms_norm(yc, norm_post)
    return x, xc


def setup_inputs(seed: int = 0) -> dict:
    key = jax.random.key(seed)
    ks = jax.random.split(key, 20)
    f32 = jnp.float32
    nrm = lambda k, shape: jax.random.normal(k, shape, dtype=f32)
    return {
        'x': nrm(ks[0], (BATCH, SEQ, D_MODEL)),
        'c': nrm(ks[1], (BATCH, D_MODEL)),
        'ctx': nrm(ks[2], (BATCH, CTX_LEN, D_MODEL)),
        'c_ctx': nrm(ks[3], (D_MODEL,)),
        'w_ada': nrm(ks[4], (DEPTH, D_MODEL, 3 * D_MODEL)) * (0.5 * D_MODEL ** -0.5),
        'b_ada': nrm(ks[5], (DEPTH, 3 * D_MODEL)) * 0.02,
        'norm_pre': 1.0 + 0.05 * nrm(ks[6], (DEPTH, D_MODEL)),
        'norm_post': 1.0 + 0.05 * nrm(ks[7], (DEPTH, D_MODEL)),
        'w_in': nrm(ks[8], (DEPTH, D_MODEL, IN_COLS)) * (D_MODEL ** -0.5),
        'w_out': nrm(ks[9], (DEPTH, MIX_WIDTH, D_MODEL)) * (MIX_WIDTH ** -0.5),
        'lam_q1': 0.1 * nrm(ks[10], (DEPTH, ATT_QK_DIM)),
        'lam_k1': 0.1 * nrm(ks[11], (DEPTH, ATT_QK_DIM)),
        'lam_q2': 0.1 * nrm(ks[12], (DEPTH, ATT_QK_DIM)),
        'lam_k2': 0.1 * nrm(ks[13], (DEPTH, ATT_QK_DIM)),
        'attn_subln': 1.0 + 0.05 * nrm(ks[14], (DEPTH, ATT_V_DIM)),
        'conv_w': nrm(ks[15], (DEPTH, CONV_WIDTH, 2 * MLSTM_HEADS * MLSTM_QK_DIM)) * (CONV_WIDTH ** -0.5),
        'conv_b': 0.02 * nrm(ks[16], (DEPTH, 2 * MLSTM_HEADS * MLSTM_QK_DIM)),
        'i_bias': 0.1 * nrm(ks[17], (DEPTH, 2, MLSTM_HEADS)),
        'f_bias': jnp.linspace(3.0, 6.0, MLSTM_HEADS, dtype=f32) + 0.1 * nrm(ks[18], (DEPTH, 2, MLSTM_HEADS)),
        'mlstm_norm': 1.0 + 0.05 * nrm(ks[19], (DEPTH, MLSTM_WIDTH)),
    }


def reference(x, c, ctx, c_ctx, w_ada, b_ada, norm_pre, norm_post, w_in, w_out,
              lam_q1, lam_k1, lam_q2, lam_k2, attn_subln, conv_w, conv_b,
              i_bias, f_bias, mlstm_norm):
    xc = ctx
    sc = jax.nn.silu(c)
    sc_ctx = jax.nn.silu(c_ctx)
    for l in range(DEPTH):
        mod = sc @ w_ada[l] + b_ada[l]
        mod_c = sc_ctx @ w_ada[l] + b_ada[l]
        x, xc = hybrid_layer(x, xc, mod, mod_c, w_in[l], w_out[l], norm_pre[l], norm_post[l],
                             lam_q1[l], lam_k1[l], lam_q2[l], lam_k2[l], attn_subln[l],
                             conv_w[l], conv_b[l], i_bias[l], f_bias[l], mlstm_norm[l],
                             l, l < DEPTH - 1)
    return x
```

```python
import functools
import math
from typing import NamedTuple

import jax
import jax.numpy as jnp
from jax import lax
from jax.experimental import pallas as pl
from jax.experimental.pallas import tpu as pltpu

F32 = jnp.float32
BF16 = jnp.bfloat16

LANES = 128
ATT_V_DIM = 128
ATT_QK_DIM = 64
MLSTM_V_DIM = 256
MLSTM_QK_DIM = 128
CONV_WIDTH = 3
GATE_SOFTCAP = 15.0
ROPE_BASE = 10000.0
GRID_W = 64
EPS = 1e-6
MOD_ROWS = 16
VMEM_LIMIT = 56 * 1024 * 1024


class Dims(NamedTuple):
    batch: int
    seq: int
    ctx: int
    d_model: int

    @property
    def att_width(self):
        return self.d_model // 2

    @property
    def att_heads(self):
        return self.att_width // ATT_V_DIM

    @property
    def ml_width(self):
        return self.d_model - self.att_width

    @property
    def ml_heads(self):
        return self.ml_width // MLSTM_V_DIM

    @property
    def n_lat(self):
        return self.batch * self.seq

    @property
    def n_tok(self):
        return self.batch * (self.seq + self.ctx)

    @property
    def col_qa(self):
        return 0

    @property
    def col_ka(self):
        return self.att_width

    @property
    def col_va(self):
        return 2 * self.att_width

    @property
    def col_ga(self):
        return 3 * self.att_width

    @property
    def col_qb(self):
        return 4 * self.att_width

    @property
    def col_kb(self):
        return self.col_qb + self.ml_heads * MLSTM_QK_DIM

    @property
    def col_vb(self):
        return self.col_kb + self.ml_heads * MLSTM_QK_DIM

    @property
    def col_ob(self):
        return self.col_vb + self.ml_width

    @property
    def col_gb(self):
        return self.col_ob + self.ml_width

    @property
    def main_cols(self):
        return self.col_gb + self.ml_width


def _params(*sem):
    return pltpu.CompilerParams(dimension_semantics=sem, vmem_limit_bytes=VMEM_LIMIT)


def _silu(x):
    return x * jax.nn.sigmoid(x)


def _ada_kernel(c_ref, w_ref, b_ref, o_ref):
    s = _silu(c_ref[...]).astype(BF16)
    w = w_ref[0].astype(BF16)
    o_ref[0] = jnp.dot(s, w, preferred_element_type=F32) + b_ref[0]


def _ada_call(c_all, w_ada, b_ada):
    depth, d, n = w_ada.shape
    tn = min(n, 1024)
    return pl.pallas_call(
        _ada_kernel,
        grid=(depth, n // tn),
        in_specs=[
            pl.BlockSpec((MOD_ROWS, d), lambda l, j: (0, 0)),
            pl.BlockSpec((1, d, tn), lambda l, j: (l, 0, j)),
            pl.BlockSpec((1, 1, tn), lambda l, j: (l, 0, j)),
        ],
        out_specs=pl.BlockSpec((1, MOD_ROWS, tn), lambda l, j: (l, 0, j)),
        out_shape=jax.ShapeDtypeStruct((depth, MOD_ROWS, n), F32),
        compiler_params=_params("arbitrary", "arbitrary"),
        name="ada",
    )(c_all, w_ada, b_ada.reshape(depth, 1, n))


def _inproj_kernel(*refs, tm, tn, rows_per_mod, fixed_mod_row, rope_tiles, aliased):
    if aliased:
        refs = refs[2:]
    (x_ref, shift_ref, scale_ref, nw_ref, w_ref, wg_ref, cos_ref, sin_ref,
     p_ref, g_ref, h_scr) = refs
    i = pl.program_id(0)
    j = pl.program_id(1)
    rc = min(tm, 256)

    @pl.when(j == 0)
    def _():
        row = fixed_mod_row if fixed_mod_row is not None else (i * tm) // rows_per_mod
        shift = shift_ref[pl.ds(row, 1), :]
        scale1 = 1.0 + scale_ref[pl.ds(row, 1), :]
        nw = nw_ref[...]
        for r in range(0, tm, rc):
            xf = x_ref[r:r + rc, :]
            ms = jnp.mean(xf * xf, axis=-1, keepdims=True)
            y = xf * lax.rsqrt(ms + EPS) * nw
            h_scr[r:r + rc, :] = (y * scale1 + shift).astype(BF16)
        g_ref[...] = jnp.dot(h_scr[...], wg_ref[...], preferred_element_type=F32)

    def project(rope):
        for n in range(0, tn, 2 * LANES):
            acc = jnp.dot(h_scr[...], w_ref[:, n:n + 2 * LANES], preferred_element_type=F32)
            if rope:
                for hh in range(2):
                    a = acc[:, hh * LANES:(hh + 1) * LANES]
                    r = a * cos_ref[...] + pltpu.roll(a, LANES // 2, axis=1) * sin_ref[...]
                    p_ref[:, n + hh * LANES:n + (hh + 1) * LANES] = r.astype(BF16)
            else:
                p_ref[:, n:n + 2 * LANES] = acc.astype(BF16)

    if rope_tiles:
        pl.when(j < rope_tiles)(lambda: project(True))
        pl.when(j >= rope_tiles)(lambda: project(False))
    else:
        project(False)


def _inproj_call(dims, x_flat, mod, nw, w_main, w_gate, cos_t, sin_t, row0, prev):
    m, d = x_flat.shape
    n_main = w_main.shape[1]
    n_gate = w_gate.shape[1]
    is_ctx = prev is not None
    tm = min(m if is_ctx else dims.seq, 1024)
    tn = min(n_main, 1024)
    assert (2 * dims.att_width) % tn == 0 and row0 % tm == 0
    tile0 = row0 // tm
    kernel = functools.partial(
        _inproj_kernel, tm=tm, tn=tn, rows_per_mod=dims.seq,
        fixed_mod_row=dims.batch if is_ctx else None,
        rope_tiles=0 if is_ctx else (2 * dims.att_width) // tn, aliased=is_ctx)
    tiles_per_seq = max(dims.seq // tm, 1)
    in_specs = [
        pl.BlockSpec((tm, d), lambda i, j: (i, 0)),
        pl.BlockSpec((MOD_ROWS, d), lambda i, j: (0, 0)),
        pl.BlockSpec((MOD_ROWS, d), lambda i, j: (0, 1)),
        pl.BlockSpec((1, d), lambda i, j: (0, 0)),
        pl.BlockSpec((d, tn), lambda i, j: (0, j)),
        pl.BlockSpec((d, n_gate), lambda i, j: (0, 0)),
        pl.BlockSpec((tm, LANES), lambda i, j: (i % tiles_per_seq, 0)),
        pl.BlockSpec((tm, LANES), lambda i, j: (i % tiles_per_seq, 0)),
    ]
    args = [x_flat, mod, mod, nw, w_main, w_gate, cos_t, sin_t]
    aliases = {}
    if is_ctx:
        in_specs = [pl.BlockSpec(memory_space=pl.ANY)] * 2 + in_specs
        args = list(prev) + args
        aliases = {0: 0, 1: 1}
    return pl.pallas_call(
        kernel,
        grid=(m // tm, n_main // tn),
        in_specs=in_specs,
        out_specs=[
            pl.BlockSpec((tm, tn), lambda i, j: (tile0 + i, j)),
            pl.BlockSpec((tm, n_gate), lambda i, j: (tile0 + i, 0)),
        ],
        out_shape=[
            jax.ShapeDtypeStruct((dims.n_tok, n_main), BF16),
            jax.ShapeDtypeStruct((dims.n_tok, n_gate), F32),
        ],
        scratch_shapes=[pltpu.VMEM((tm, d), BF16)],
        input_output_aliases=aliases,
        compiler_params=_params("arbitrary", "arbitrary"),
        name="in_proj_ctx" if is_ctx else "in_proj",
    )(*args)


def _attn_kernel(*refs, lam_init, has_lat, aliased):
    if aliased:
        refs = refs[1:]
    if has_lat:
        q_ref, kc_ref, vc_ref, kl_ref, vl_ref, g_ref, lam_ref, sub_ref, o_ref = refs
    else:
        q_ref, kc_ref, vc_ref, g_ref, lam_ref, sub_ref, o_ref = refs
    lp = lam_ref[...]
    lam = (jnp.exp(jnp.sum(lp[0:1] * lp[1:2], axis=1, keepdims=True))
           - jnp.exp(jnp.sum(lp[2:3] * lp[3:4], axis=1, keepdims=True)) + lam_init)
    q = q_ref[...]
    lane = lax.broadcasted_iota(jnp.int32, (1, LANES), 1)
    first = (lane % (LANES // 2)) < (LANES // 4)
    nt = (((1,), (1,)), ((), ()))
    outs = []
    for sub in range(2):
        qs = jnp.where(first if sub == 0 else jnp.logical_not(first), q, jnp.zeros_like(q))
        s_c = lax.dot_general(qs, kc_ref[...], nt, preferred_element_type=F32)
        m = jnp.max(s_c, axis=1, keepdims=True)
        if has_lat:
            s_l = lax.dot_general(qs, kl_ref[...], nt, preferred_element_type=F32)
            m = jnp.maximum(m, jnp.max(s_l, axis=1, keepdims=True))
        p_c = jnp.exp(s_c - m)
        den = jnp.sum(p_c, axis=1, keepdims=True)
        o = jnp.dot(p_c.astype(BF16), vc_ref[...], preferred_element_type=F32)
        if has_lat:
            p_l = jnp.exp(s_l - m)
            den = den + jnp.sum(p_l, axis=1, keepdims=True)
            o = o + jnp.dot(p_l.astype(BF16), vl_ref[...], preferred_element_type=F32)
        outs.append(o / den)
    o = outs[0] - lam * outs[1]
    ms = jnp.mean(o * o, axis=1, keepdims=True)
    y = o * lax.rsqrt(ms + EPS) * sub_ref[...] * (1.0 - lam_init)
    o_ref[...] = (y * _silu(g_ref[...].astype(F32))).astype(BF16)


def _attn_call(dims, p, lam_rows, subln, layer_idx, y_prev):
    lam_init = 0.8 - 0.6 * math.exp(-0.3 * layer_idx)
    is_ctx = y_prev is not None
    ah = dims.att_heads
    seq, ctx = dims.seq, dims.ctx
    ctx_blk0 = dims.n_lat // ctx
    cb = lambda off: off // LANES
    if is_ctx:
        tq, nq = ctx, 1
        qrow = lambda b, h, t: ctx_blk0 + b
    else:
        tq = min(seq, 512)
        nq = seq // tq
        qrow = lambda b, h, t: b * nq + t
    in_specs = [
        pl.BlockSpec((tq, LANES), lambda b, h, t: (qrow(b, h, t), cb(dims.col_qa) + h)),
        pl.BlockSpec((ctx, LANES), lambda b, h, t: (ctx_blk0 + b, cb(dims.col_ka) + h)),
        pl.BlockSpec((ctx, LANES), lambda b, h, t: (ctx_blk0 + b, cb(dims.col_va) + h)),
    ]
    args = [p, p, p]
    if not is_ctx:
        in_specs += [
            pl.BlockSpec((seq, LANES), lambda b, h, t: (b, cb(dims.col_ka) + h)),
            pl.BlockSpec((seq, LANES), lambda b, h, t: (b, cb(dims.col_va) + h)),
        ]
        args += [p, p]
    in_specs += [
        pl.BlockSpec((tq, LANES), lambda b, h, t: (qrow(b, h, t), cb(dims.col_ga) + h)),
        pl.BlockSpec((4, ATT_QK_DIM), lambda b, h, t: (0, 0)),
        pl.BlockSpec((1, ATT_V_DIM), lambda b, h, t: (0, 0)),
    ]
    args += [p, lam_rows, subln]
    aliases = {}
    if is_ctx:
        in_specs = [pl.BlockSpec(memory_space=pl.ANY)] + in_specs
        args = [y_prev] + args
        aliases = {0: 0}
    return pl.pallas_call(
        functools.partial(_attn_kernel, lam_init=lam_init, has_lat=not is_ctx, aliased=is_ctx),
        grid=(dims.batch, ah, nq),
        in_specs=in_specs,
        out_specs=pl.BlockSpec((tq, LANES), lambda b, h, t: (qrow(b, h, t), h)),
        out_shape=jax.ShapeDtypeStruct((dims.n_tok, dims.d_model), BF16),
        input_output_aliases=aliases,
        compiler_params=_params("arbitrary", "arbitrary", "arbitrary"),
        name="attn_ctx" if is_ctx else "attn",
    )(*args)


def _split3(x):
    hi = x.astype(BF16)
    r1 = x - hi.astype(F32)
    mid = r1.astype(BF16)
    lo = (r1 - mid.astype(F32)).astype(BF16)
    return hi, mid, lo


def _conv_silu(src_ref, w, b, scale, dst_ref):
    x = src_ref[...].astype(F32)
    t = x.shape[0]
    row = lax.broadcasted_iota(jnp.int32, (t, 1), 0)
    prev = jnp.where(row == 0, 0.0, pltpu.roll(x, 1, axis=0))
    nxt = jnp.where(row == t - 1, 0.0, pltpu.roll(x, t - 1, axis=0))
    y = b + w[0:1] * prev
    y = y + w[1:2] * x
    y = y + w[2:3] * nxt
    y = _silu(y)
    if scale is not None:
        y = y * scale
    dst_ref[...] = y.astype(BF16)


def _mlstm_kernel(*refs, chunk, n_lat_chunks, with_ctx_out, aliased):
    if aliased:
        refs = refs[1:]
    (ql_ref, kl_ref, vl_ref, gl_ref, ol_ref, sl_ref,
     qc_ref, kc_ref, vc_ref, gc_ref, oc_ref, sc_ref,
     cwq_ref, cwk_ref, cbq_ref, cbk_ref, ib_ref, fb_ref, hn_ref) = refs[:19]
    refs = refs[19:]
    if with_ctx_out:
        yl_ref, yc_ref = refs[:2]
        refs = refs[2:]
    else:
        yl_ref, yc_ref = refs[0], None
        refs = refs[1:]
    qsl, ksl, qsc, ksc, hfl, hbl, hfc, hbc, c_scr, n_scr, m_scr = refs
    L = chunk
    hd = pl.program_id(1)

    k_scale = MLSTM_QK_DIM ** -0.5
    _conv_silu(ql_ref, cwq_ref[...], cbq_ref[...], None, qsl)
    _conv_silu(kl_ref, cwk_ref[...], cbk_ref[...], k_scale, ksl)
    _conv_silu(qc_ref, cwq_ref[...], cbq_ref[...], None, qsc)
    _conv_silu(kc_ref, cwk_ref[...], cbk_ref[...], k_scale, ksc)

    c_scr[...] = jnp.zeros_like(c_scr)
    n_scr[...] = jnp.zeros_like(n_scr)
    m_scr[...] = jnp.zeros_like(m_scr)

    lane = lax.broadcasted_iota(jnp.int32, (1, LANES), 1)
    bias = jnp.where(lane == 0, ib_ref[0, hd],
                     jnp.where(lane == 1, ib_ref[1, hd],
                               jnp.where(lane == 2, fb_ref[0, hd],
                                         jnp.where(lane == 3, fb_ref[1, hd], 0.0))))
    ri = lax.broadcasted_iota(jnp.int32, (L, L), 0)
    ci = lax.broadcasted_iota(jnp.int32, (L, L), 1)
    tri = (ci <= ri).astype(BF16)
    nt = (((1,), (1,)), ((), ()))
    tn = (((0,), (0,)), ((), ()))

    def gate_tiles(g_raw):
        x = g_raw + bias
        capped = GATE_SOFTCAP * jnp.tanh(x * (1.0 / GATE_SOFTCAP))
        lsig = jnp.minimum(capped, 0.0) - jnp.log1p(jnp.exp(-jnp.abs(capped)))
        lf = jnp.where(lane >= 2, lsig, 0.0)
        hi, mid, lo = _split3(lf)
        prefix = (jnp.dot(tri, hi, preferred_element_type=F32)
                  + jnp.dot(tri, mid, preferred_element_type=F32)
                  + jnp.dot(tri, lo, preferred_element_type=F32))
        tot = prefix[L - 1:L, :]
        suffix = tot - prefix + lf
        r = jnp.where(lane < 2, capped, jnp.where(lane == 2, prefix, suffix))
        return r, r.T, tot

    def one_direction(d, q, k, v, r, rt, tot):
        icol = r[:, d:d + 1]
        bcol = r[:, 2 + d:3 + d]
        irow = rt[d:d + 1, :]
        brow = rt[2 + d:3 + d, :]
        blast = tot[:, 2 + d:3 + d]
        m_prev = m_scr[d, 0:1, 0:1]
        c_prev = c_scr[d]
        n_prev = n_scr[d, 0:1, :]
        mask = (ci <= ri) if d == 0 else (ci >= ri)
        logw = jnp.where(mask, bcol + (irow - brow), -jnp.inf)
        inter = bcol + m_prev
        m_row = jnp.maximum(inter, jnp.max(logw, axis=1, keepdims=True))
        w = jnp.exp(logw - m_row)
        s = lax.dot_general(q, k, nt, preferred_element_type=F32) * w
        sc = jnp.exp(inter - m_row)
        numer = (sc * jnp.dot(q, c_prev.astype(BF16), preferred_element_type=F32)
                 + jnp.dot(s.astype(BF16), v, preferred_element_type=F32))
        qn = jnp.sum(q.astype(F32) * n_prev, axis=1, keepdims=True)
        denom = sc * qn + jnp.sum(s, axis=1, keepdims=True)
        h = numer / jnp.maximum(jnp.abs(denom), jnp.exp(-m_row))
        gcol = blast - bcol + icol
        m_new = jnp.maximum(blast + m_prev, jnp.max(gcol, axis=0, keepdims=True))
        wk = jnp.exp(gcol - m_new)
        decay = jnp.exp(blast + m_prev - m_new)
        kw = wk * k.astype(F32)
        c_scr[d] = decay * c_prev + lax.dot_general(kw.astype(BF16), v, tn, preferred_element_type=F32)
        n_scr[d] = jnp.broadcast_to(decay * n_prev + jnp.sum(kw, axis=0, keepdims=True), n_scr.shape[1:])
        m_scr[d] = jnp.broadcast_to(m_new, m_scr.shape[1:])
        return h

    n_ctx_chunks = qsc.shape[0] // L
    for jc in range(n_ctx_chunks):
        for d, c in ((0, jc), (1, n_ctx_chunks - 1 - jc)):
            rows = slice(c * L, (c + 1) * L)
            r, rt, tot = gate_tiles(gc_ref[rows, :])
            h = one_direction(d, qsc[rows, :], ksc[rows, :], vc_ref[rows, :], r, rt, tot)
            (hfc if d == 0 else hbc)[rows, :] = h

    def lat_step(jj, carry):
        for d in range(2):
            c = jj if d == 0 else n_lat_chunks - 1 - jj
            rows = pl.ds(pl.multiple_of(c * L, L), L)
            r, rt, tot = gate_tiles(gl_ref[rows, :])
            h = one_direction(d, qsl[rows, :], ksl[rows, :], vl_ref[rows, :], r, rt, tot)
            (hfl if d == 0 else hbl)[rows, :] = h
        return carry

    lax.fori_loop(0, n_lat_chunks, lat_step, 0)

    hn = hn_ref[...]

    def finish(hf, hb, o_ref, s_ref, y_ref, rows):
        hsum = hf[rows, :] + hb[rows, :]
        ms = jnp.mean(hsum * hsum, axis=1, keepdims=True)
        y = hsum * lax.rsqrt(ms + EPS) * hn
        y = y * jax.nn.sigmoid(o_ref[rows, :].astype(F32))
        y_ref[rows, :] = (y * _silu(s_ref[rows, :].astype(F32))).astype(BF16)

    def lat_finish(jj, carry):
        finish(hfl, hbl, ol_ref, sl_ref, yl_ref, pl.ds(pl.multiple_of(jj * L, L), L))
        return carry

    lax.fori_loop(0, n_lat_chunks, lat_finish, 0)
    if with_ctx_out:
        for jc in range(n_ctx_chunks):
            finish(hfc, hbc, oc_ref, sc_ref, yc_ref, slice(jc * L, (jc + 1) * L))


def _mlstm_call(dims, p, g, y_attn, conv_w, conv_b, i_bias, f_bias, head_norm, with_ctx_out):
    mh = dims.ml_heads
    seq, ctx = dims.seq, dims.ctx
    L = min(ctx, 256)
    ctx_blk0 = dims.n_lat // ctx
    qk, dv = MLSTM_QK_DIM, MLSTM_V_DIM
    lat = lambda width, off: pl.BlockSpec((seq, width), lambda b, h: (b, off // width + h))
    cx = lambda width, off: pl.BlockSpec((ctx, width), lambda b, h: (ctx_blk0 + b, off // width + h))
    smem = pl.BlockSpec(memory_space=pltpu.SMEM)
    in_specs = [
        pl.BlockSpec(memory_space=pl.ANY),
        lat(qk, dims.col_qb), lat(qk, dims.col_kb), lat(dv, dims.col_vb), lat(LANES, 0),
        lat(dv, dims.col_ob), lat(dv, dims.col_gb),
        cx(qk, dims.col_qb), cx(qk, dims.col_kb), cx(dv, dims.col_vb), cx(LANES, 0),
        cx(dv, dims.col_ob), cx(dv, dims.col_gb),
        pl.BlockSpec((CONV_WIDTH, qk), lambda b, h: (0, h)),
        pl.BlockSpec((CONV_WIDTH, qk), lambda b, h: (0, mh + h)),
        pl.BlockSpec((1, qk), lambda b, h: (0, h)),
        pl.BlockSpec((1, qk), lambda b, h: (0, mh + h)),
        smem, smem,
        pl.BlockSpec((1, dv), lambda b, h: (0, h)),
    ]
    args = [y_attn, p, p, p, g, p, p, p, p, p, g, p, p,
            conv_w, conv_w, conv_b, conv_b, i_bias, f_bias, head_norm]
    y_col0 = dims.att_width // dv
    out_specs = [pl.BlockSpec((seq, dv), lambda b, h: (b, y_col0 + h))]
    out_shape = [jax.ShapeDtypeStruct((dims.n_tok, dims.d_model), BF16)]
    if with_ctx_out:
        out_specs.append(pl.BlockSpec((ctx, dv), lambda b, h: (b, h)))
        out_shape.append(jax.ShapeDtypeStruct((dims.batch * ctx, dims.ml_width), BF16))
    scratch = [
        pltpu.VMEM((seq, qk), BF16), pltpu.VMEM((seq, qk), BF16),
        pltpu.VMEM((ctx, qk), BF16), pltpu.VMEM((ctx, qk), BF16),
        pltpu.VMEM((seq, dv), F32), pltpu.VMEM((seq, dv), F32),
        pltpu.VMEM((ctx, dv), F32), pltpu.VMEM((ctx, dv), F32),
        pltpu.VMEM((2, qk, dv), F32), pltpu.VMEM((2, 8, qk), F32), pltpu.VMEM((2, 8, LANES), F32),
    ]
    return pl.pallas_call(
        functools.partial(_mlstm_kernel, chunk=L, n_lat_chunks=seq // L,
                          with_ctx_out=with_ctx_out, aliased=True),
        grid=(dims.batch, mh),
        in_specs=in_specs,
        out_specs=out_specs,
        out_shape=out_shape,
        scratch_shapes=scratch,
        input_output_aliases={0: 0},
        compiler_params=_params("arbitrary", "arbitrary"),
        name="mlstm",
    )(*args)


def _outproj_kernel(y_ref, w_ref, x_ref, gate_ref, nw_ref, o_ref, *, tm, rows_per_mod, fixed_mod_row):
    i = pl.program_id(0)
    d = o_ref.shape[1]
    nc = min(d, 512)
    ss = jnp.zeros((tm, 1), F32)
    for n in range(0, d, nc):
        acc = jnp.dot(y_ref[...], w_ref[:, n:n + nc], preferred_element_type=F32)
        ss = ss + jnp.sum(acc * acc, axis=1, keepdims=True)
        o_ref[:, n:n + nc] = acc
    row = fixed_mod_row if fixed_mod_row is not None else (i * tm) // rows_per_mod
    gate = gate_ref[pl.ds(row, 1), :]
    inv = lax.rsqrt(ss * (1.0 / d) + EPS)
    o_ref[...] = x_ref[...] + gate * (o_ref[...] * inv * nw_ref[...])


def _outproj_call(dims, y, w_out, x_flat, mod, nw, row0, is_ctx):
    m, d = x_flat.shape
    tm = min(m, 512)
    tile0 = row0 // tm
    kernel = functools.partial(_outproj_kernel, tm=tm, rows_per_mod=dims.seq,
                               fixed_mod_row=dims.batch if is_ctx else None)
    return pl.pallas_call(
        kernel,
        grid=(m // tm,),
        in_specs=[
            pl.BlockSpec((tm, d), lambda i: (tile0 + i, 0)),
            pl.BlockSpec((d, d), lambda i: (0, 0)),
            pl.BlockSpec((tm, d), lambda i: (i, 0)),
            pl.BlockSpec((MOD_ROWS, d), lambda i: (0, 2)),
            pl.BlockSpec((1, d), lambda i: (0, 0)),
        ],
        out_specs=pl.BlockSpec((tm, d), lambda i: (i, 0)),
        out_shape=jax.ShapeDtypeStruct((m, d), F32),
        compiler_params=_params("arbitrary"),
        name="out_proj_ctx" if is_ctx else "out_proj",
    )(y, w_out, x_flat, mod, nw)


def _rope_tables(seq):
    rows = seq // GRID_W
    row = jnp.repeat(jnp.arange(rows), GRID_W).astype(F32)
    col = jnp.tile(jnp.arange(GRID_W), rows).astype(F32)
    n_freq = ATT_QK_DIM // 4
    inv = ROPE_BASE ** (-jnp.arange(n_freq, dtype=F32) / n_freq)
    ang = jnp.concatenate([row[:, None] * inv, col[:, None] * inv], axis=-1)
    cos, sin = jnp.cos(ang), jnp.sin(ang)
    return jnp.tile(cos, (1, 4)), jnp.concatenate([-sin, -sin, sin, sin], axis=-1)


def _layout_weights(dims, w_in_l):
    d = dims.d_model
    ah, mh = dims.att_heads, dims.ml_heads

    def regroup(w):
        w = w.reshape(d, ah, 2, ATT_QK_DIM // 2, 2)
        return w.transpose(0, 1, 4, 2, 3).reshape(d, ah * 2 * ATT_QK_DIM)

    aw = dims.att_width
    wq = regroup(w_in_l[:, :aw]) * (ATT_QK_DIM ** -0.5)
    wk = regroup(w_in_l[:, aw:2 * aw])
    w_main = jnp.concatenate([wq, wk, w_in_l[:, 2 * aw:dims.main_cols]], axis=1).astype(BF16)
    wg = w_in_l[:, dims.main_cols:].reshape(d, 4, mh).transpose(0, 2, 1)
    wg = jnp.pad(wg, ((0, 0), (0, 0), (0, LANES - 4))).reshape(d, mh * LANES).astype(BF16)
    return w_main, wg


def _forward(dims, x, c, ctx, c_ctx, w_ada, b_ada, norm_pre, norm_post, w_in, w_out,
             lam_q1, lam_k1, lam_q2, lam_k2, attn_subln, conv_w, conv_b, i_bias, f_bias, mlstm_norm):
    depth = w_ada.shape[0]
    d = dims.d_model
    x_flat = x.reshape(dims.n_lat, d)
    xc_flat = ctx.reshape(dims.batch * dims.ctx, d)
    c_all = jnp.concatenate(
        [c, c_ctx[None], jnp.zeros((MOD_ROWS - dims.batch - 1, d), F32)], axis=0)
    mod_all = _ada_call(c_all, w_ada, b_ada)
    cos_t, sin_t = _rope_tables(dims.seq)
    for l in range(depth):
        need_ctx = l < depth - 1
        mod = mod_all[l]
        w_main, w_gate = _layout_weights(dims, w_in[l])
        nw_pre = norm_pre[l].reshape(1, d)
        pg = _inproj_call(dims, x_flat, mod, nw_pre, w_main, w_gate, cos_t, sin_t, 0, None)
        p, g = _inproj_call(dims, xc_flat, mod, nw_pre, w_main, w_gate, cos_t, sin_t, dims.n_lat, pg)
        lam_rows = jnp.stack([lam_q1[l], lam_k1[l], lam_q2[l], lam_k2[l]])
        subln = attn_subln[l].reshape(1, ATT_V_DIM)
        y = _attn_call(dims, p, lam_rows, subln, l, None)
        if need_ctx:
            y = _attn_call(dims, p, lam_rows, subln, l, y)
        outs = _mlstm_call(dims, p, g, y, conv_w[l], conv_b[l].reshape(1, -1), i_bias[l], f_bias[l],
                           mlstm_norm[l].reshape(1, -1), need_ctx)
        y = outs[0]
        w_out_l = w_out[l].astype(BF16)
        nw_post = norm_post[l].reshape(1, d)
        x_new = _outproj_call(dims, y, w_out_l, x_flat, mod, nw_post, 0, False)
        if need_ctx:
            yc = jnp.concatenate([y[dims.n_lat:, :dims.att_width], outs[1]], axis=1)
            xc_flat = _outproj_call(dims, yc, w_out_l, xc_flat, mod, nw_post, 0, True)
        x_flat = x_new
    return x_flat.reshape(x.shape)


def kernel(x, c, ctx, c_ctx, w_ada, b_ada, norm_pre, norm_post, w_in, w_out, lam_q1, lam_k1, lam_q2, lam_k2,
           attn_subln, conv_w, conv_b, i_bias, f_bias, mlstm_norm):
    dims = Dims(batch=x.shape[0], seq=x.shape[1], ctx=ctx.shape[1], d_model=x.shape[2])
    return _forward(dims, x, c, ctx, c_ctx, w_ada, b_ada, norm_pre, norm_post, w_in, w_out,
                    lam_q1, lam_k1, lam_q2, lam_k2, attn_subln, conv_w, conv_b, i_bias, f_bias, mlstm_norm)
```

```python
import functools
import math
from typing import NamedTuple

import jax
import jax.numpy as jnp
from jax import lax
from jax.experimental import pallas as pl
from jax.experimental.pallas import tpu as pltpu

F32 = jnp.float32
BF16 = jnp.bfloat16

LANES = 128
ATT_V_DIM = 128
ATT_QK_DIM = 64
MLSTM_V_DIM = 256
MLSTM_QK_DIM = 128
CONV_WIDTH = 3
GATE_SOFTCAP = 15.0
ROPE_BASE = 10000.0
GRID_W = 64
EPS = 1e-6
MOD_ROWS = 16
VMEM_LIMIT = 56 * 1024 * 1024


class Dims(NamedTuple):
    batch: int
    seq: int
    ctx: int
    d_model: int

    @property
    def att_width(self):
        return self.d_model // 2

    @property
    def att_heads(self):
        return self.att_width // ATT_V_DIM

    @property
    def ml_width(self):
        return self.d_model - self.att_width

    @property
    def ml_heads(self):
        return self.ml_width // MLSTM_V_DIM

    @property
    def n_lat(self):
        return self.batch * self.seq

    @property
    def n_tok(self):
        return self.batch * (self.seq + self.ctx)

    @property
    def col_qa(self):
        return 0

    @property
    def col_ka(self):
        return self.att_width

    @property
    def col_va(self):
        return 2 * self.att_width

    @property
    def col_ga(self):
        return 3 * self.att_width

    @property
    def col_qb(self):
        return 4 * self.att_width

    @property
    def col_kb(self):
        return self.col_qb + self.ml_heads * MLSTM_QK_DIM

    @property
    def col_vb(self):
        return self.col_kb + self.ml_heads * MLSTM_QK_DIM

    @property
    def col_ob(self):
        return self.col_vb + self.ml_width

    @property
    def col_gb(self):
        return self.col_ob + self.ml_width

    @property
    def main_cols(self):
        return self.col_gb + self.ml_width


def _params(*sem):
    return pltpu.CompilerParams(dimension_semantics=sem, vmem_limit_bytes=VMEM_LIMIT)


def _silu(x):
    return x * jax.nn.sigmoid(x)


def _ada_kernel(c_ref, w_ref, b_ref, o_ref):
    s = _silu(c_ref[...]).astype(BF16)
    w = w_ref[0].astype(BF16)
    o_ref[0] = jnp.dot(s, w, preferred_element_type=F32) + b_ref[0]


def _ada_call(c_all, w_ada, b_ada):
    depth, d, n = w_ada.shape
    tn = min(n, 1024)
    return pl.pallas_call(
        _ada_kernel,
        grid=(depth, n // tn),
        in_specs=[
            pl.BlockSpec((MOD_ROWS, d), lambda l, j: (0, 0)),
            pl.BlockSpec((1, d, tn), lambda l, j: (l, 0, j)),
            pl.BlockSpec((1, 1, tn), lambda l, j: (l, 0, j)),
        ],
        out_specs=pl.BlockSpec((1, MOD_ROWS, tn), lambda l, j: (l, 0, j)),
        out_shape=jax.ShapeDtypeStruct((depth, MOD_ROWS, n), F32),
        compiler_params=_params("arbitrary", "arbitrary"),
        name="ada",
    )(c_all, w_ada, b_ada.reshape(depth, 1, n))


def _inproj_kernel(*refs, tm, tn, rows_per_mod, fixed_mod_row, rope_tiles, aliased):
    if aliased:
        refs = refs[2:]
    (x_ref, shift_ref, scale_ref, nw_ref, w_ref, wg_ref, cos_ref, sin_ref,
     p_ref, g_ref, h_scr) = refs
    i = pl.program_id(0)
    j = pl.program_id(1)
    rc = min(tm, 256)

    @pl.when(j == 0)
    def _():
        row = fixed_mod_row if fixed_mod_row is not None else (i * tm) // rows_per_mod
        shift = shift_ref[pl.ds(row, 1), :]
        scale1 = 1.0 + scale_ref[pl.ds(row, 1), :]
        nw = nw_ref[...]
        for r in range(0, tm, rc):
            xf = x_ref[r:r + rc, :]
            ms = jnp.mean(xf * xf, axis=-1, keepdims=True)
            y = xf * lax.rsqrt(ms + EPS) * nw
            h_scr[r:r + rc, :] = (y * scale1 + shift).astype(BF16)
        g_ref[...] = jnp.dot(h_scr[...], wg_ref[...], preferred_element_type=F32)

    def project(rope):
        for n in range(0, tn, 2 * LANES):
            acc = jnp.dot(h_scr[...], w_ref[:, n:n + 2 * LANES], preferred_element_type=F32)
            if rope:
                for hh in range(2):
                    a = acc[:, hh * LANES:(hh + 1) * LANES]
                    r = a * cos_ref[...] + pltpu.roll(a, LANES // 2, axis=1) * sin_ref[...]
                    p_ref[:, n + hh * LANES:n + (hh + 1) * LANES] = r.astype(BF16)
            else:
                p_ref[:, n:n + 2 * LANES] = acc.astype(BF16)

    if rope_tiles:
        pl.when(j < rope_tiles)(lambda: project(True))
        pl.when(j >= rope_tiles)(lambda: project(False))
    else:
        project(False)


def _inproj_call(dims, x_flat, mod, nw, w_main, w_gate, cos_t, sin_t, row0, prev):
    m, d = x_flat.shape
    n_main = w_main.shape[1]
    n_gate = w_gate.shape[1]
    is_ctx = prev is not None
    tm = min(m if is_ctx else dims.seq, 1024)
    tn = min(n_main, 1024)
    assert (2 * dims.att_width) % tn == 0 and row0 % tm == 0
    tile0 = row0 // tm
    kernel = functools.partial(
        _inproj_kernel, tm=tm, tn=tn, rows_per_mod=dims.seq,
        fixed_mod_row=dims.batch if is_ctx else None,
        rope_tiles=0 if is_ctx else (2 * dims.att_width) // tn, aliased=is_ctx)
    tiles_per_seq = max(dims.seq // tm, 1)
    in_specs = [
        pl.BlockSpec((tm, d), lambda i, j: (i, 0)),
        pl.BlockSpec((MOD_ROWS, d), lambda i, j: (0, 0)),
        pl.BlockSpec((MOD_ROWS, d), lambda i, j: (0, 1)),
        pl.BlockSpec((1, d), lambda i, j: (0, 0)),
        pl.BlockSpec((d, tn), lambda i, j: (0, j)),
        pl.BlockSpec((d, n_gate), lambda i, j: (0, 0)),
        pl.BlockSpec((tm, LANES), lambda i, j: (i % tiles_per_seq, 0)),
        pl.BlockSpec((tm, LANES), lambda i, j: (i % tiles_per_seq, 0)),
    ]
    args = [x_flat, mod, mod, nw, w_main, w_gate, cos_t, sin_t]
    aliases = {}
    if is_ctx:
        in_specs = [pl.BlockSpec(memory_space=pl.ANY)] * 2 + in_specs
        args = list(prev) + args
        aliases = {0: 0, 1: 1}
    return pl.pallas_call(
        kernel,
        grid=(m // tm, n_main // tn),
        in_specs=in_specs,
        out_specs=[
            pl.BlockSpec((tm, tn), lambda i, j: (tile0 + i, j)),
            pl.BlockSpec((tm, n_gate), lambda i, j: (tile0 + i, 0)),
        ],
        out_shape=[
            jax.ShapeDtypeStruct((dims.n_tok, n_main), BF16),
            jax.ShapeDtypeStruct((dims.n_tok, n_gate), F32),
        ],
        scratch_shapes=[pltpu.VMEM((tm, d), BF16)],
        input_output_aliases=aliases,
        compiler_params=_params("arbitrary", "arbitrary"),
        name="in_proj_ctx" if is_ctx else "in_proj",
    )(*args)


def _transpose_rows_into(src_ref, dst_ref, rows):
    for r in range(0, src_ref.shape[0], rows):
        dst_ref[:, r:r + rows] = src_ref[r:r + rows, :].astype(F32).T.astype(BF16)


def _attn_kernel(*refs, lam_init, has_lat, aliased):
    if aliased:
        refs = refs[1:]
    if has_lat:
        q_ref, kc_ref, vc_ref, kl_ref, vl_ref, g_ref, lam_ref, sub_ref, o_ref, vtc_scr, vtl_scr = refs
    else:
        q_ref, kc_ref, vc_ref, g_ref, lam_ref, sub_ref, o_ref, vtc_scr = refs

    @pl.when(pl.program_id(2) == 0)
    def _():
        _transpose_rows_into(vc_ref, vtc_scr, min(vc_ref.shape[0], 256))
        if has_lat:
            _transpose_rows_into(vl_ref, vtl_scr, min(vl_ref.shape[0], 256))

    lp = lam_ref[...]
    lam = (jnp.exp(jnp.sum(lp[0:1] * lp[1:2], axis=1, keepdims=True))
           - jnp.exp(jnp.sum(lp[2:3] * lp[3:4], axis=1, keepdims=True)) + lam_init)
    q = q_ref[...]
    lane = lax.broadcasted_iota(jnp.int32, (1, LANES), 1)
    first = (lane % (LANES // 2)) < (LANES // 4)
    nt = (((1,), (1,)), ((), ()))
    outs = []
    for sub in range(2):
        qs = jnp.where(first if sub == 0 else jnp.logical_not(first), q, jnp.zeros_like(q))
        s_c = lax.dot_general(kc_ref[...], qs, nt, preferred_element_type=F32)
        m = jnp.max(s_c, axis=0, keepdims=True)
        if has_lat:
            s_l = lax.dot_general(kl_ref[...], qs, nt, preferred_element_type=F32)
            m = jnp.maximum(m, jnp.max(s_l, axis=0, keepdims=True))
        p_c = jnp.exp2(s_c - m)
        den = jnp.sum(p_c, axis=0, keepdims=True)
        o = jnp.dot(vtc_scr[...], p_c.astype(BF16), preferred_element_type=F32)
        if has_lat:
            p_l = jnp.exp2(s_l - m)
            den = den + jnp.sum(p_l, axis=0, keepdims=True)
            o = o + jnp.dot(vtl_scr[...], p_l.astype(BF16), preferred_element_type=F32)
        outs.append(o / den)
    o = outs[0] - lam * outs[1]
    ms = jnp.mean(o * o, axis=0, keepdims=True)
    y = (o * lax.rsqrt(ms + EPS)).T
    y = y * sub_ref[...] * (1.0 - lam_init)
    o_ref[...] = (y * _silu(g_ref[...].astype(F32))).astype(BF16)


def _attn_call(dims, p, lam_rows, subln, layer_idx, y_prev):
    lam_init = 0.8 - 0.6 * math.exp(-0.3 * layer_idx)
    is_ctx = y_prev is not None
    ah = dims.att_heads
    seq, ctx = dims.seq, dims.ctx
    ctx_blk0 = dims.n_lat // ctx
    cb = lambda off: off // LANES
    if is_ctx:
        tq, nq = ctx, 1
        qrow = lambda b, h, t: ctx_blk0 + b
    else:
        tq = min(seq, 512)
        nq = seq // tq
        qrow = lambda b, h, t: b * nq + t
    in_specs = [
        pl.BlockSpec((tq, LANES), lambda b, h, t: (qrow(b, h, t), cb(dims.col_qa) + h)),
        pl.BlockSpec((ctx, LANES), lambda b, h, t: (ctx_blk0 + b, cb(dims.col_ka) + h)),
        pl.BlockSpec((ctx, LANES), lambda b, h, t: (ctx_blk0 + b, cb(dims.col_va) + h)),
    ]
    args = [p, p, p]
    if not is_ctx:
        in_specs += [
            pl.BlockSpec((seq, LANES), lambda b, h, t: (b, cb(dims.col_ka) + h)),
            pl.BlockSpec((seq, LANES), lambda b, h, t: (b, cb(dims.col_va) + h)),
        ]
        args += [p, p]
    in_specs += [
        pl.BlockSpec((tq, LANES), lambda b, h, t: (qrow(b, h, t), cb(dims.col_ga) + h)),
        pl.BlockSpec((4, ATT_QK_DIM), lambda b, h, t: (0, 0)),
        pl.BlockSpec((1, ATT_V_DIM), lambda b, h, t: (0, 0)),
    ]
    args += [p, lam_rows, subln]
    aliases = {}
    if is_ctx:
        in_specs = [pl.BlockSpec(memory_space=pl.ANY)] + in_specs
        args = [y_prev] + args
        aliases = {0: 0}
    return pl.pallas_call(
        functools.partial(_attn_kernel, lam_init=lam_init, has_lat=not is_ctx, aliased=is_ctx),
        grid=(dims.batch, ah, nq),
        in_specs=in_specs,
        out_specs=pl.BlockSpec((tq, LANES), lambda b, h, t: (qrow(b, h, t), h)),
        out_shape=jax.ShapeDtypeStruct((dims.n_tok, dims.d_model), BF16),
        scratch_shapes=[pltpu.VMEM((ATT_V_DIM, ctx), BF16)]
        + ([] if is_ctx else [pltpu.VMEM((ATT_V_DIM, seq), BF16)]),
        input_output_aliases=aliases,
        compiler_params=_params("arbitrary", "arbitrary", "arbitrary"),
        name="attn_ctx" if is_ctx else "attn",
    )(*args)


def _split3(x):
    hi = x.astype(BF16)
    r1 = x - hi.astype(F32)
    mid = r1.astype(BF16)
    lo = (r1 - mid.astype(F32)).astype(BF16)
    return hi, mid, lo


def _conv_silu(src_ref, w, b, scale, dst_ref):
    x = src_ref[...].astype(F32)
    t = x.shape[0]
    row = lax.broadcasted_iota(jnp.int32, (t, 1), 0)
    prev = jnp.where(row == 0, 0.0, pltpu.roll(x, 1, axis=0))
    nxt = jnp.where(row == t - 1, 0.0, pltpu.roll(x, t - 1, axis=0))
    y = b + w[0:1] * prev
    y = y + w[1:2] * x
    y = y + w[2:3] * nxt
    y = _silu(y)
    if scale is not None:
        y = y * scale
    dst_ref[...] = y.astype(BF16)


def _mlstm_kernel(*refs, chunk, n_lat_chunks, with_ctx_out, aliased):
    if aliased:
        refs = refs[1:]
    (ql_ref, kl_ref, vl_ref, gl_ref, ol_ref, sl_ref,
     qc_ref, kc_ref, vc_ref, gc_ref, oc_ref, sc_ref,
     cwq_ref, cwk_ref, cbq_ref, cbk_ref, ib_ref, fb_ref, hn_ref) = refs[:19]
    refs = refs[19:]
    if with_ctx_out:
        yl_ref, yc_ref = refs[:2]
        refs = refs[2:]
    else:
        yl_ref, yc_ref = refs[0], None
        refs = refs[1:]
    qsl, ksl, qsc, ksc, vtl, vtc, hfl, hbl, hfc, hbc, ct_scr, n_scr, m_scr = refs
    L = chunk
    hd = pl.program_id(1)
    n_ctx_chunks = qsc.shape[0] // L

    k_scale = MLSTM_QK_DIM ** -0.5
    _conv_silu(ql_ref, cwq_ref[...], cbq_ref[...], None, qsl)
    _conv_silu(kl_ref, cwk_ref[...], cbk_ref[...], k_scale, ksl)
    _conv_silu(qc_ref, cwq_ref[...], cbq_ref[...], None, qsc)
    _conv_silu(kc_ref, cwk_ref[...], cbk_ref[...], k_scale, ksc)

    def v_transpose(jj, carry):
        rows = pl.ds(pl.multiple_of(jj * L, L), L)
        vtl[jj] = vl_ref[rows, :].astype(F32).T.astype(BF16)
        return carry

    lax.fori_loop(0, n_lat_chunks, v_transpose, 0)
    for jc in range(n_ctx_chunks):
        vtc[jc] = vc_ref[jc * L:(jc + 1) * L, :].astype(F32).T.astype(BF16)

    ct_scr[...] = jnp.zeros_like(ct_scr)
    n_scr[...] = jnp.zeros_like(n_scr)
    m_scr[...] = jnp.zeros_like(m_scr)

    lane = lax.broadcasted_iota(jnp.int32, (1, LANES), 1)
    bias = jnp.where(lane == 0, ib_ref[0, hd],
                     jnp.where(lane == 1, ib_ref[1, hd],
                               jnp.where(lane == 2, fb_ref[0, hd],
                                         jnp.where(lane == 3, fb_ref[1, hd], 0.0))))
    ri = lax.broadcasted_iota(jnp.int32, (L, L), 0)
    ci = lax.broadcasted_iota(jnp.int32, (L, L), 1)
    tri = (ci <= ri).astype(BF16)
    nt = (((1,), (1,)), ((), ()))

    def gate_tiles(g_raw):
        x = g_raw + bias
        capped = GATE_SOFTCAP * jnp.tanh(x * (1.0 / GATE_SOFTCAP))
        lsig = jnp.minimum(capped, 0.0) - jnp.log1p(jnp.exp(-jnp.abs(capped)))
        lf = jnp.where(lane >= 2, lsig, 0.0)
        hi, mid, lo = _split3(lf)
        prefix = (jnp.dot(tri, hi, preferred_element_type=F32)
                  + jnp.dot(tri, mid, preferred_element_type=F32)
                  + jnp.dot(tri, lo, preferred_element_type=F32))
        tot = prefix[L - 1:L, :]
        suffix = tot - prefix + lf
        r = jnp.where(lane < 2, capped, jnp.where(lane == 2, prefix, suffix))
        return r, r.T, tot

    def one_direction(d, q, k, vt, r, rt, tot):
        dcol = r[:, d:d + 1] - r[:, 2 + d:3 + d]
        irow = rt[d:d + 1, :]
        brow = rt[2 + d:3 + d, :]
        blast = tot[:, 2 + d:3 + d]
        m_prev = m_scr[d, 0:1, 0:1]
        ct_prev = ct_scr[d]
        n_prev = n_scr[d]
        mask = (ri <= ci) if d == 0 else (ri >= ci)
        logw = jnp.where(mask, brow + dcol, -jnp.inf)
        inter = brow + m_prev
        m_t = jnp.maximum(inter, jnp.max(logw, axis=0, keepdims=True))
        w = jnp.exp(logw - m_t)
        s = lax.dot_general(k, q, nt, preferred_element_type=F32) * w
        sc = jnp.exp(inter - m_t)
        numer = (sc * lax.dot_general(ct_prev.astype(BF16), q, nt, preferred_element_type=F32)
                 + jnp.dot(vt, s.astype(BF16), preferred_element_type=F32))
        qn = lax.dot_general(n_prev.astype(BF16), q, nt, preferred_element_type=F32)[0:1, :]
        denom = sc * qn + jnp.sum(s, axis=0, keepdims=True)
        h = numer * (1.0 / jnp.maximum(jnp.abs(denom), jnp.exp(-m_t)))
        grow = blast - brow + irow
        m_new = jnp.maximum(blast + m_prev, jnp.max(grow, axis=1, keepdims=True))
        wk = jnp.exp(grow - m_new)
        decay = jnp.exp(blast + m_prev - m_new)
        vw = (vt.astype(F32) * wk).astype(BF16)
        ct_scr[d] = decay * ct_prev + jnp.dot(vw, k, preferred_element_type=F32)
        wk8 = jnp.broadcast_to(wk, (8, L)).astype(BF16)
        n_scr[d] = decay * n_prev + jnp.dot(wk8, k, preferred_element_type=F32)
        m_scr[d] = jnp.broadcast_to(m_new, m_scr.shape[1:])
        return h

    for jc in range(n_ctx_chunks):
        for d, c in ((0, jc), (1, n_ctx_chunks - 1 - jc)):
            rows = slice(c * L, (c + 1) * L)
            r, rt, tot = gate_tiles(gc_ref[rows, :])
            (hfc if d == 0 else hbc)[c] = one_direction(d, qsc[rows, :], ksc[rows, :], vtc[c], r, rt, tot)

    def lat_step(jj, carry):
        for d in range(2):
            c = jj if d == 0 else n_lat_chunks - 1 - jj
            rows = pl.ds(pl.multiple_of(c * L, L), L)
            r, rt, tot = gate_tiles(gl_ref[rows, :])
            (hfl if d == 0 else hbl)[c] = one_direction(d, qsl[rows, :], ksl[rows, :], vtl[c], r, rt, tot)
        return carry

    lax.fori_loop(0, n_lat_chunks, lat_step, 0)

    hn = hn_ref[...]

    def finish(hf, hb, o_ref, s_ref, y_ref, c, rows):
        hsum = hf[c] + hb[c]
        ms = jnp.mean(hsum * hsum, axis=0, keepdims=True)
        y = (hsum * lax.rsqrt(ms + EPS)).T * hn
        y = y * jax.nn.sigmoid(o_ref[rows, :].astype(F32))
        y_ref[rows, :] = (y * _silu(s_ref[rows, :].astype(F32))).astype(BF16)

    def lat_finish(jj, carry):
        finish(hfl, hbl, ol_ref, sl_ref, yl_ref, jj, pl.ds(pl.multiple_of(jj * L, L), L))
        return carry

    lax.fori_loop(0, n_lat_chunks, lat_finish, 0)
    if with_ctx_out:
        for jc in range(n_ctx_chunks):
            finish(hfc, hbc, oc_ref, sc_ref, yc_ref, jc, slice(jc * L, (jc + 1) * L))


def _mlstm_call(dims, p, g, y_attn, conv_w, conv_b, i_bias, f_bias, head_norm, with_ctx_out):
    mh = dims.ml_heads
    seq, ctx = dims.seq, dims.ctx
    L = min(ctx, 256)
    ctx_blk0 = dims.n_lat // ctx
    qk, dv = MLSTM_QK_DIM, MLSTM_V_DIM
    lat = lambda width, off: pl.BlockSpec((seq, width), lambda b, h: (b, off // width + h))
    cx = lambda width, off: pl.BlockSpec((ctx, width), lambda b, h: (ctx_blk0 + b, off // width + h))
    smem = pl.BlockSpec(memory_space=pltpu.SMEM)
    in_specs = [
        pl.BlockSpec(memory_space=pl.ANY),
        lat(qk, dims.col_qb), lat(qk, dims.col_kb), lat(dv, dims.col_vb), lat(LANES, 0),
        lat(dv, dims.col_ob), lat(dv, dims.col_gb),
        cx(qk, dims.col_qb), cx(qk, dims.col_kb), cx(dv, dims.col_vb), cx(LANES, 0),
        cx(dv, dims.col_ob), cx(dv, dims.col_gb),
        pl.BlockSpec((CONV_WIDTH, qk), lambda b, h: (0, h)),
        pl.BlockSpec((CONV_WIDTH, qk), lambda b, h: (0, mh + h)),
        pl.BlockSpec((1, qk), lambda b, h: (0, h)),
        pl.BlockSpec((1, qk), lambda b, h: (0, mh + h)),
        smem, smem,
        pl.BlockSpec((1, dv), lambda b, h: (0, h)),
    ]
    args = [y_attn, p, p, p, g, p, p, p, p, p, g, p, p,
            conv_w, conv_w, conv_b, conv_b, i_bias, f_bias, head_norm]
    y_col0 = dims.att_width // dv
    out_specs = [pl.BlockSpec((seq, dv), lambda b, h: (b, y_col0 + h))]
    out_shape = [jax.ShapeDtypeStruct((dims.n_tok, dims.d_model), BF16)]
    if with_ctx_out:
        out_specs.append(pl.BlockSpec((ctx, dv), lambda b, h: (b, h)))
        out_shape.append(jax.ShapeDtypeStruct((dims.batch * ctx, dims.ml_width), BF16))
    nlc, ncc = seq // L, ctx // L
    scratch = [
        pltpu.VMEM((seq, qk), BF16), pltpu.VMEM((seq, qk), BF16),
        pltpu.VMEM((ctx, qk), BF16), pltpu.VMEM((ctx, qk), BF16),
        pltpu.VMEM((nlc, dv, L), BF16), pltpu.VMEM((ncc, dv, L), BF16),
        pltpu.VMEM((nlc, dv, L), F32), pltpu.VMEM((nlc, dv, L), F32),
        pltpu.VMEM((ncc, dv, L), F32), pltpu.VMEM((ncc, dv, L), F32),
        pltpu.VMEM((2, dv, qk), F32), pltpu.VMEM((2, 8, qk), F32), pltpu.VMEM((2, 8, LANES), F32),
    ]
    return pl.pallas_call(
        functools.partial(_mlstm_kernel, chunk=L, n_lat_chunks=seq // L,
                          with_ctx_out=with_ctx_out, aliased=True),
        grid=(dims.batch, mh),
        in_specs=in_specs,
        out_specs=out_specs,
        out_shape=out_shape,
        scratch_shapes=scratch,
        input_output_aliases={0: 0},
        compiler_params=_params("arbitrary", "arbitrary"),
        name="mlstm",
    )(*args)


def _outproj_kernel(y_ref, w_ref, x_ref, gate_ref, nw_ref, o_ref, *, tm, rows_per_mod, fixed_mod_row):
    i = pl.program_id(0)
    d = o_ref.shape[1]
    nc = min(d, 512)
    ss = jnp.zeros((tm, 1), F32)
    for n in range(0, d, nc):
        acc = jnp.dot(y_ref[...], w_ref[:, n:n + nc], preferred_element_type=F32)
        ss = ss + jnp.sum(acc * acc, axis=1, keepdims=True)
        o_ref[:, n:n + nc] = acc
    row = fixed_mod_row if fixed_mod_row is not None else (i * tm) // rows_per_mod
    gate = gate_ref[pl.ds(row, 1), :]
    inv = lax.rsqrt(ss * (1.0 / d) + EPS)
    o_ref[...] = x_ref[...] + gate * (o_ref[...] * inv * nw_ref[...])


def _outproj_call(dims, y, w_out, x_flat, mod, nw, row0, is_ctx):
    m, d = x_flat.shape
    tm = min(m, 512)
    tile0 = row0 // tm
    kernel = functools.partial(_outproj_kernel, tm=tm, rows_per_mod=dims.seq,
                               fixed_mod_row=dims.batch if is_ctx else None)
    return pl.pallas_call(
        kernel,
        grid=(m // tm,),
        in_specs=[
            pl.BlockSpec((tm, d), lambda i: (tile0 + i, 0)),
            pl.BlockSpec((d, d), lambda i: (0, 0)),
            pl.BlockSpec((tm, d), lambda i: (i, 0)),
            pl.BlockSpec((MOD_ROWS, d), lambda i: (0, 2)),
            pl.BlockSpec((1, d), lambda i: (0, 0)),
        ],
        out_specs=pl.BlockSpec((tm, d), lambda i: (i, 0)),
        out_shape=jax.ShapeDtypeStruct((m, d), F32),
        compiler_params=_params("arbitrary"),
        name="out_proj_ctx" if is_ctx else "out_proj",
    )(y, w_out, x_flat, mod, nw)


def _rope_tables(seq):
    rows = seq // GRID_W
    row = jnp.repeat(jnp.arange(rows), GRID_W).astype(F32)
    col = jnp.tile(jnp.arange(GRID_W), rows).astype(F32)
    n_freq = ATT_QK_DIM // 4
    inv = ROPE_BASE ** (-jnp.arange(n_freq, dtype=F32) / n_freq)
    ang = jnp.concatenate([row[:, None] * inv, col[:, None] * inv], axis=-1)
    cos, sin = jnp.cos(ang), jnp.sin(ang)
    return jnp.tile(cos, (1, 4)), jnp.concatenate([-sin, -sin, sin, sin], axis=-1)


def _layout_weights(dims, w_in_l):
    d = dims.d_model
    ah, mh = dims.att_heads, dims.ml_heads

    def regroup(w):
        w = w.reshape(d, ah, 2, ATT_QK_DIM // 2, 2)
        return w.transpose(0, 1, 4, 2, 3).reshape(d, ah * 2 * ATT_QK_DIM)

    aw = dims.att_width
    wq = regroup(w_in_l[:, :aw]) * (ATT_QK_DIM ** -0.5 * math.log2(math.e))
    wk = regroup(w_in_l[:, aw:2 * aw])
    w_main = jnp.concatenate([wq, wk, w_in_l[:, 2 * aw:dims.main_cols]], axis=1).astype(BF16)
    wg = w_in_l[:, dims.main_cols:].reshape(d, 4, mh).transpose(0, 2, 1)
    wg = jnp.pad(wg, ((0, 0), (0, 0), (0, LANES - 4))).reshape(d, mh * LANES).astype(BF16)
    return w_main, wg


def _forward(dims, x, c, ctx, c_ctx, w_ada, b_ada, norm_pre, norm_post, w_in, w_out,
             lam_q1, lam_k1, lam_q2, lam_k2, attn_subln, conv_w, conv_b, i_bias, f_bias, mlstm_norm):
    depth = w_ada.shape[0]
    d = dims.d_model
    x_flat = x.reshape(dims.n_lat, d)
    xc_flat = ctx.reshape(dims.batch * dims.ctx, d)
    c_all = jnp.concatenate(
        [c, c_ctx[None], jnp.zeros((MOD_ROWS - dims.batch - 1, d), F32)], axis=0)
    mod_all = _ada_call(c_all, w_ada, b_ada)
    cos_t, sin_t = _rope_tables(dims.seq)
    for l in range(depth):
        need_ctx = l < depth - 1
        mod = mod_all[l]
        w_main, w_gate = _layout_weights(dims, w_in[l])
        nw_pre = norm_pre[l].reshape(1, d)
        pg = _inproj_call(dims, x_flat, mod, nw_pre, w_main, w_gate, cos_t, sin_t, 0, None)
        p, g = _inproj_call(dims, xc_flat, mod, nw_pre, w_main, w_gate, cos_t, sin_t, dims.n_lat, pg)
        lam_rows = jnp.stack([lam_q1[l], lam_k1[l], lam_q2[l], lam_k2[l]])
        subln = attn_subln[l].reshape(1, ATT_V_DIM)
        y = _attn_call(dims, p, lam_rows, subln, l, None)
        if need_ctx:
            y = _attn_call(dims, p, lam_rows, subln, l, y)
        outs = _mlstm_call(dims, p, g, y, conv_w[l], conv_b[l].reshape(1, -1), i_bias[l], f_bias[l],
                           mlstm_norm[l].reshape(1, -1), need_ctx)
        y = outs[0]
        w_out_l = w_out[l].astype(BF16)
        nw_post = norm_post[l].reshape(1, d)
        x_new = _outproj_call(dims, y, w_out_l, x_flat, mod, nw_post, 0, False)
        if need_ctx:
            yc = jnp.concatenate([y[dims.n_lat:, :dims.att_width], outs[1]], axis=1)
            xc_flat = _outproj_call(dims, yc, w_out_l, xc_flat, mod, nw_post, 0, True)
        x_flat = x_new
    return x_flat.reshape(x.shape)


def kernel(x, c, ctx, c_ctx, w_ada, b_ada, norm_pre, norm_post, w_in, w_out, lam_q1, lam_k1, lam_q2, lam_k2,
           attn_subln, conv_w, conv_b, i_bias, f_bias, mlstm_norm):
    dims = Dims(batch=x.shape[0], seq=x.shape[1], ctx=ctx.shape[1], d_model=x.shape[2])
    return _forward(dims, x, c, ctx, c_ctx, w_ada, b_ada, norm_pre, norm_post, w_in, w_out,
                    lam_q1, lam_k1, lam_q2, lam_k2, attn_subln, conv_w, conv_b, i_bias, f_bias, mlstm_norm)
```

```python
import functools
import math
from typing import NamedTuple

import numpy as np
import jax
import jax.numpy as jnp
from jax import lax
from jax.experimental import pallas as pl
from jax.experimental.pallas import tpu as pltpu

F32 = jnp.float32
BF16 = jnp.bfloat16

LANES = 128
ATT_V_DIM = 128
ATT_QK_DIM = 64
MLSTM_V_DIM = 256
MLSTM_QK_DIM = 128
CONV_WIDTH = 3
GATE_SOFTCAP = 15.0
ROPE_BASE = 10000.0
GRID_W = 64
EPS = 1e-6
MOD_ROWS = 16
VMEM_LIMIT = 56 * 1024 * 1024
NT = (((1,), (1,)), ((), ()))


class Dims(NamedTuple):
    batch: int
    seq: int
    ctx: int
    d_model: int

    @property
    def att_width(self):
        return self.d_model // 2

    @property
    def att_heads(self):
        return self.att_width // ATT_V_DIM

    @property
    def ml_width(self):
        return self.d_model - self.att_width

    @property
    def ml_heads(self):
        return self.ml_width // MLSTM_V_DIM

    @property
    def n_lat(self):
        return self.batch * self.seq

    @property
    def n_ctx(self):
        return self.batch * self.ctx

    @property
    def col_qa(self):
        return 0

    @property
    def col_ka(self):
        return self.att_width

    @property
    def col_va(self):
        return 2 * self.att_width

    @property
    def col_ga(self):
        return 3 * self.att_width

    @property
    def col_qb(self):
        return 4 * self.att_width

    @property
    def col_kb(self):
        return self.col_qb + self.ml_heads * MLSTM_QK_DIM

    @property
    def col_vb(self):
        return self.col_kb + self.ml_heads * MLSTM_QK_DIM

    @property
    def col_ob(self):
        return self.col_vb + self.ml_width

    @property
    def col_gb(self):
        return self.col_ob + self.ml_width

    @property
    def main_cols(self):
        return self.col_gb + self.ml_width


def _params(*sem):
    return pltpu.CompilerParams(dimension_semantics=sem, vmem_limit_bytes=VMEM_LIMIT)


def _silu(x):
    return x * jax.nn.sigmoid(x)


def _wprep_kernel(w_ref, perm_ref, o_ref, *, qk_tiles, q_tiles, q_scale):
    j = pl.program_id(2)

    @pl.when(j < qk_tiles)
    def _():
        x = (w_ref[0] * jnp.where(j < q_tiles, q_scale, 1.0)).astype(BF16)
        for c in range(0, x.shape[1], LANES):
            o_ref[0, :, c:c + LANES] = jnp.dot(
                x[:, c:c + LANES], perm_ref[...], preferred_element_type=F32).astype(BF16)

    @pl.when(j >= qk_tiles)
    def _():
        o_ref[0] = w_ref[0].astype(BF16)


def _rope_lane_permutation():
    perm = np.zeros((LANES, LANES), np.float32)
    for c in range(2):
        for i in range(ATT_QK_DIM // 2):
            for par in range(2):
                perm[c * ATT_QK_DIM + 2 * i + par, par * ATT_QK_DIM + c * (ATT_QK_DIM // 2) + i] = 1.0
    return jnp.asarray(perm, BF16)


def _wprep_call(dims, w_in):
    depth, d, _ = w_in.shape
    tn = min(dims.att_width, 1024)
    tr = min(d, 512)
    assert dims.att_width % tn == 0 and dims.main_cols % tn == 0
    kernel = functools.partial(_wprep_kernel, qk_tiles=2 * dims.att_width // tn, q_tiles=dims.att_width // tn,
                               q_scale=ATT_QK_DIM ** -0.5 * math.log2(math.e))
    return pl.pallas_call(
        kernel,
        grid=(depth, d // tr, dims.main_cols // tn),
        in_specs=[
            pl.BlockSpec((1, tr, tn), lambda l, i, j: (l, i, j)),
            pl.BlockSpec((LANES, LANES), lambda l, i, j: (0, 0)),
        ],
        out_specs=pl.BlockSpec((1, tr, tn), lambda l, i, j: (l, i, j)),
        out_shape=jax.ShapeDtypeStruct((depth, d, dims.main_cols), BF16),
        compiler_params=_params("arbitrary", "arbitrary", "arbitrary"),
        name="w_prep",
    )(w_in, _rope_lane_permutation())


def _cast_kernel(w_ref, o_ref):
    o_ref[...] = w_ref[...].astype(BF16)


def _cast_call(w):
    depth, r, c = w.shape
    tr = min(r, 512)
    return pl.pallas_call(
        _cast_kernel,
        grid=(depth, r // tr),
        in_specs=[pl.BlockSpec((1, tr, c), lambda l, i: (l, i, 0))],
        out_specs=pl.BlockSpec((1, tr, c), lambda l, i: (l, i, 0)),
        out_shape=jax.ShapeDtypeStruct(w.shape, BF16),
        compiler_params=_params("arbitrary", "arbitrary"),
        name="w_cast",
    )(w)


def _ada_kernel(c_ref, w_ref, b_ref, o_ref):
    s = _silu(c_ref[...]).astype(BF16)
    w = w_ref[0].astype(BF16)
    o_ref[0] = jnp.dot(s, w, preferred_element_type=F32) + b_ref[0]


def _ada_call(c_all, w_ada, b_ada):
    depth, d, n = w_ada.shape
    tn = min(n, 1024)
    return pl.pallas_call(
        _ada_kernel,
        grid=(depth, n // tn),
        in_specs=[
            pl.BlockSpec((MOD_ROWS, d), lambda l, j: (0, 0)),
            pl.BlockSpec((1, d, tn), lambda l, j: (l, 0, j)),
            pl.BlockSpec((1, 1, tn), lambda l, j: (l, 0, j)),
        ],
        out_specs=pl.BlockSpec((1, MOD_ROWS, tn), lambda l, j: (l, 0, j)),
        out_shape=jax.ShapeDtypeStruct((depth, MOD_ROWS, n), F32),
        compiler_params=_params("arbitrary", "arbitrary"),
        name="ada",
    )(c_all, w_ada, b_ada.reshape(depth, 1, n))


def _inproj_kernel(x_ref, shift_ref, scale_ref, nw_ref, w_ref, wg_ref, cos_ref, sin_ref, p_ref, g_ref, h_scr,
                   *, tm, tn, rows_per_mod, fixed_mod_row, rope_tiles):
    i = pl.program_id(0)
    j = pl.program_id(1)
    rc = min(tm, 256)

    @pl.when(j == 0)
    def _():
        row = fixed_mod_row if fixed_mod_row is not None else (i * tm) // rows_per_mod
        shift = shift_ref[pl.ds(row, 1), :]
        scale1 = 1.0 + scale_ref[pl.ds(row, 1), :]
        nw = nw_ref[...]
        for r in range(0, tm, rc):
            xf = x_ref[r:r + rc, :]
            ms = jnp.mean(xf * xf, axis=-1, keepdims=True)
            y = xf * lax.rsqrt(ms + EPS) * nw
            h_scr[r:r + rc, :] = (y * scale1 + shift).astype(BF16)
        g_ref[...] = jnp.dot(h_scr[...], wg_ref[...], preferred_element_type=F32)

    def project(rope):
        for n in range(0, tn, 2 * LANES):
            acc = jnp.dot(h_scr[...], w_ref[:, n:n + 2 * LANES], preferred_element_type=F32)
            if rope:
                for hh in range(2):
                    a = acc[:, hh * LANES:(hh + 1) * LANES]
                    r = a * cos_ref[...] + pltpu.roll(a, LANES // 2, axis=1) * sin_ref[...]
                    p_ref[:, n + hh * LANES:n + (hh + 1) * LANES] = r.astype(BF16)
            else:
                p_ref[:, n:n + 2 * LANES] = acc.astype(BF16)

    if rope_tiles:
        pl.when(j < rope_tiles)(lambda: project(True))
        pl.when(j >= rope_tiles)(lambda: project(False))
    else:
        project(False)


def _inproj_call(dims, x_flat, mod, nw, w_main, layer, w_gate, cos_t, sin_t, is_ctx):
    m, d = x_flat.shape
    n_main = w_main.shape[2]
    n_gate = w_gate.shape[1]
    tm = min(m if is_ctx else dims.seq, 1024)
    tn = min(n_main, 1024)
    assert (2 * dims.att_width) % tn == 0
    kernel = functools.partial(
        _inproj_kernel, tm=tm, tn=tn, rows_per_mod=dims.seq,
        fixed_mod_row=dims.batch if is_ctx else None,
        rope_tiles=0 if is_ctx else (2 * dims.att_width) // tn)
    tiles_per_seq = max(dims.seq // tm, 1)
    return pl.pallas_call(
        kernel,
        grid=(m // tm, n_main // tn),
        in_specs=[
            pl.BlockSpec((tm, d), lambda i, j: (i, 0)),
            pl.BlockSpec((MOD_ROWS, d), lambda i, j: (0, 0)),
            pl.BlockSpec((MOD_ROWS, d), lambda i, j: (0, 1)),
            pl.BlockSpec((1, d), lambda i, j: (0, 0)),
            pl.BlockSpec((None, d, tn), lambda i, j: (layer, 0, j)),
            pl.BlockSpec((d, n_gate), lambda i, j: (0, 0)),
            pl.BlockSpec((tm, LANES), lambda i, j: (i % tiles_per_seq, 0)),
            pl.BlockSpec((tm, LANES), lambda i, j: (i % tiles_per_seq, 0)),
        ],
        out_specs=[
            pl.BlockSpec((tm, tn), lambda i, j: (i, j)),
            pl.BlockSpec((tm, n_gate), lambda i, j: (i, 0)),
        ],
        out_shape=[
            jax.ShapeDtypeStruct((m, n_main), BF16),
            jax.ShapeDtypeStruct((m, n_gate), F32),
        ],
        scratch_shapes=[pltpu.VMEM((tm, d), BF16)],
        compiler_params=_params("arbitrary", "arbitrary"),
        name="in_proj_ctx" if is_ctx else "in_proj",
    )(x_flat, mod, mod, nw, w_main, w_gate, cos_t, sin_t)


def _lambda(lam_ref, lam_init):
    lp = lam_ref[...]
    return (jnp.exp(jnp.sum(lp[0:1] * lp[1:2], axis=1, keepdims=True))
            - jnp.exp(jnp.sum(lp[2:3] * lp[3:4], axis=1, keepdims=True)) + lam_init)


def _first_sub_head_lanes():
    lane = lax.broadcasted_iota(jnp.int32, (1, LANES), 1)
    return (lane % (LANES // 2)) < (LANES // 4)


def _attn_finish(o_t, lam_init, sub_ref, g):
    ms = jnp.mean(o_t * o_t, axis=0, keepdims=True)
    y = (o_t * lax.rsqrt(ms + EPS)).T
    y = y * sub_ref[...] * (1.0 - lam_init)
    return (y * _silu(g.astype(F32))).astype(BF16)


def _attn_lat_kernel(q_ref, kc_ref, vc_ref, kl_ref, vl_ref, g_ref, lam_ref, sub_ref, o_ref,
                     vt_scr, s_scr, *, lam_init, tq, kc):
    ctx, seq = kc_ref.shape[0], kl_ref.shape[0]
    n_tiles = seq // tq
    n_kc = (ctx + seq) // kc

    for r in range(0, ctx, kc):
        vt_scr[:, r:r + kc] = vc_ref[r:r + kc, :].astype(F32).T.astype(BF16)
    for r in range(0, seq, kc):
        vt_scr[:, ctx + r:ctx + r + kc] = vl_ref[r:r + kc, :].astype(F32).T.astype(BF16)

    lam = _lambda(lam_ref, lam_init)
    first = _first_sub_head_lanes()

    def unit_queries(u):
        t, sub = divmod(u, 2)
        q = q_ref[t * tq:(t + 1) * tq, :]
        return jnp.where(first if sub == 0 else jnp.logical_not(first), q, jnp.zeros_like(q))

    def scores(qs, slot):
        s_c = lax.dot_general(kc_ref[...], qs, NT, preferred_element_type=F32)
        s_l = lax.dot_general(kl_ref[...], qs, NT, preferred_element_type=F32)
        s_scr[slot, 0:ctx] = s_c
        s_scr[slot, ctx:] = s_l
        return jnp.maximum(jnp.max(s_c, axis=0, keepdims=True), jnp.max(s_l, axis=0, keepdims=True))

    def softmax_values(slot, m):
        den = jnp.zeros((1, tq), F32)
        o_t = jnp.zeros((ATT_V_DIM, tq), F32)
        for i in range(n_kc):
            p = jnp.exp2(s_scr[slot, i * kc:(i + 1) * kc, :] - m)
            den = den + jnp.sum(p, axis=0, keepdims=True)
            o_t = o_t + jnp.dot(vt_scr[:, i * kc:(i + 1) * kc], p.astype(BF16), preferred_element_type=F32)
        return o_t * (1.0 / den)

    n_units = 2 * n_tiles
    m = scores(unit_queries(0), 0)
    o_first = None
    for u in range(n_units):
        m_next = scores(unit_queries(u + 1), (u + 1) % 2) if u + 1 < n_units else None
        o_t = softmax_values(u % 2, m)
        m = m_next
        if u % 2 == 0:
            o_first = o_t
        else:
            rows = slice((u // 2) * tq, (u // 2 + 1) * tq)
            o_ref[rows, :] = _attn_finish(o_first - lam * o_t, lam_init, sub_ref, g_ref[rows, :])


def _attn_ctx_kernel(q_ref, kc_ref, vc_ref, g_ref, lam_ref, sub_ref, o_ref, *, lam_init):
    lam = _lambda(lam_ref, lam_init)
    first = _first_sub_head_lanes()
    q = q_ref[...]
    vt = vc_ref[...].astype(F32).T.astype(BF16)
    outs = []
    for sub in range(2):
        qs = jnp.where(first if sub == 0 else jnp.logical_not(first), q, jnp.zeros_like(q))
        s = lax.dot_general(kc_ref[...], qs, NT, preferred_element_type=F32)
        p = jnp.exp2(s - jnp.max(s, axis=0, keepdims=True))
        den = jnp.sum(p, axis=0, keepdims=True)
        outs.append(jnp.dot(vt, p.astype(BF16), preferred_element_type=F32) * (1.0 / den))
    o_ref[...] = _attn_finish(outs[0] - lam * outs[1], lam_init, sub_ref, g_ref[...])


def _attn_call(dims, p_lat, p_ctx, lam_rows, subln, layer_idx, is_ctx):
    lam_init = 0.8 - 0.6 * math.exp(-0.3 * layer_idx)
    ah = dims.att_heads
    seq, ctx = dims.seq, dims.ctx
    cb = lambda off: off // LANES
    lat = lambda off: pl.BlockSpec((seq, LANES), lambda b, h: (b, cb(off) + h))
    cx = lambda off: pl.BlockSpec((ctx, LANES), lambda b, h: (b, cb(off) + h))
    small = [pl.BlockSpec((4, ATT_QK_DIM), lambda b, h: (0, 0)),
             pl.BlockSpec((1, ATT_V_DIM), lambda b, h: (0, 0))]
    if is_ctx:
        return pl.pallas_call(
            functools.partial(_attn_ctx_kernel, lam_init=lam_init),
            grid=(dims.batch, ah),
            in_specs=[cx(dims.col_qa), cx(dims.col_ka), cx(dims.col_va), cx(dims.col_ga)] + small,
            out_specs=pl.BlockSpec((ctx, LANES), lambda b, h: (b, h)),
            out_shape=jax.ShapeDtypeStruct((dims.n_ctx, dims.att_width), BF16),
            compiler_params=_params("arbitrary", "arbitrary"),
            name="attn_ctx",
        )(p_ctx, p_ctx, p_ctx, p_ctx, lam_rows, subln)
    tq = min(seq, 512)
    kc = min(ctx, 256)
    return pl.pallas_call(
        functools.partial(_attn_lat_kernel, lam_init=lam_init, tq=tq, kc=kc),
        grid=(dims.batch, ah),
        in_specs=[lat(dims.col_qa), cx(dims.col_ka), cx(dims.col_va), lat(dims.col_ka), lat(dims.col_va),
                  lat(dims.col_ga)] + small,
        out_specs=pl.BlockSpec((seq, LANES), lambda b, h: (b, h)),
        out_shape=jax.ShapeDtypeStruct((dims.n_lat, dims.att_width), BF16),
        scratch_shapes=[
            pltpu.VMEM((ATT_V_DIM, ctx + seq), BF16),
            pltpu.VMEM((2, ctx + seq, tq), F32),
        ],
        compiler_params=_params("arbitrary", "arbitrary"),
        name="attn",
    )(p_lat, p_ctx, p_ctx, p_lat, p_lat, p_lat, lam_rows, subln)


def _split3(x):
    hi = x.astype(BF16)
    r1 = x - hi.astype(F32)
    mid = r1.astype(BF16)
    lo = (r1 - mid.astype(F32)).astype(BF16)
    return hi, mid, lo


def _conv_silu(src_ref, w, b, scale, dst_ref):
    x = src_ref[...].astype(F32)
    t = x.shape[0]
    row = lax.broadcasted_iota(jnp.int32, (t, 1), 0)
    prev = jnp.where(row == 0, 0.0, pltpu.roll(x, 1, axis=0))
    nxt = jnp.where(row == t - 1, 0.0, pltpu.roll(x, t - 1, axis=0))
    y = b + w[0:1] * prev
    y = y + w[1:2] * x
    y = y + w[2:3] * nxt
    y = _silu(y)
    if scale is not None:
        y = y * scale
    dst_ref[...] = y.astype(BF16)


def _mlstm_kernel(*refs, chunk, with_ctx_out):
    (ql_ref, kl_ref, vl_ref, gl_ref, ol_ref, sl_ref,
     qc_ref, kc_ref, vc_ref, gc_ref, oc_ref, sc_ref,
     cwq_ref, cwk_ref, cbq_ref, cbk_ref, ib_ref, fb_ref, hn_ref) = refs[:19]
    refs = refs[19:]
    if with_ctx_out:
        yl_ref, yc_ref = refs[:2]
        refs = refs[2:]
    else:
        yl_ref, yc_ref = refs[0], None
        refs = refs[1:]
    qs_scr, ks_scr, vt_scr, gr_scr, h_scr, u_scr = refs
    L = chunk
    ctx, seq = qc_ref.shape[0], ql_ref.shape[0]
    ncc, nc = ctx // L, (ctx + seq) // L
    hd = pl.program_id(1)

    k_scale = MLSTM_QK_DIM ** -0.5
    _conv_silu(qc_ref, cwq_ref[...], cbq_ref[...], None, qs_scr.at[0:ctx])
    _conv_silu(kc_ref, cwk_ref[...], cbk_ref[...], k_scale, ks_scr.at[0:ctx])
    _conv_silu(ql_ref, cwq_ref[...], cbq_ref[...], None, qs_scr.at[ctx:])
    _conv_silu(kl_ref, cwk_ref[...], cbk_ref[...], k_scale, ks_scr.at[ctx:])

    def src_rows(c):
        return (True, slice(c * L, (c + 1) * L)) if c < ncc else (False, slice((c - ncc) * L, (c - ncc + 1) * L))

    for c in range(nc):
        is_ctx, rows = src_rows(c)
        vt_scr[c] = (vc_ref if is_ctx else vl_ref)[rows, :].astype(F32).T.astype(BF16)
        gr_scr[c] = (gc_ref if is_ctx else gl_ref)[rows, :].T[0:8, :]

    row = lax.broadcasted_iota(jnp.int32, (1, 8, 1), 1)
    bias = jnp.where(row == 0, ib_ref[0, hd],
                     jnp.where(row == 1, ib_ref[1, hd],
                               jnp.where(row == 2, fb_ref[0, hd],
                                         jnp.where(row == 3, fb_ref[1, hd], 0.0))))
    capped = GATE_SOFTCAP * jnp.tanh((gr_scr[...] + bias) * (1.0 / GATE_SOFTCAP))
    lsig = jnp.minimum(capped, 0.0) - jnp.log1p(jnp.exp(-jnp.abs(capped)))
    lf = jnp.where((row >= 2) & (row < 4), lsig, 0.0).reshape(nc * 8, L)
    ui = lax.broadcasted_iota(jnp.int32, (L, L), 0)
    ti = lax.broadcasted_iota(jnp.int32, (L, L), 1)
    tri_t = (ui <= ti).astype(BF16)
    hi, mid, lo = _split3(lf)
    prefix = (jnp.dot(hi, tri_t, preferred_element_type=F32) + jnp.dot(mid, tri_t, preferred_element_type=F32)
              + jnp.dot(lo, tri_t, preferred_element_type=F32))
    tot = prefix[:, L - 1:L]
    suffix = tot - prefix + lf
    row2 = lax.broadcasted_iota(jnp.int32, (nc * 8, 1), 0) % 8
    gates = jnp.where(row2 < 2, capped.reshape(nc * 8, L), jnp.where(row2 == 2, prefix, suffix))

    mask_f = ui <= ti
    mask_b = ui >= ti
    zeros_pad = jnp.zeros((LANES - 8, L), F32)

    rows_of = []
    local = [[None] * nc for _ in range(2)]
    for c in range(nc):
        g8 = gates[c * 8:(c + 1) * 8, :]
        t8 = tot[c * 8:(c + 1) * 8, :]
        dcols = jnp.concatenate([g8 - pltpu.roll(g8, 6, axis=0), zeros_pad], axis=0).T
        q = qs_scr[c * L:(c + 1) * L, :]
        k = ks_scr[c * L:(c + 1) * L, :]
        vt = vt_scr[c]
        qk = lax.dot_general(k, q, NT, preferred_element_type=F32)
        per_dir = []
        for d in range(2):
            irow, brow, blast = g8[d:d + 1, :], g8[2 + d:3 + d, :], t8[2 + d:3 + d, :]
            per_dir.append((irow, brow, blast))
            logw = jnp.where(mask_f if d == 0 else mask_b, brow + dcols[:, d:d + 1], -jnp.inf)
            a = jnp.max(logw, axis=0, keepdims=True)
            s = qk * jnp.exp(logw - a)
            h_scr[d, c] = jnp.dot(vt, s.astype(BF16), preferred_element_type=F32)
            dsum = jnp.sum(s, axis=0, keepdims=True)
            grow = blast - brow + irow
            gmax = jnp.max(grow, axis=1, keepdims=True)
            wk = jnp.exp(grow - gmax)
            u_scr[d, c] = jnp.dot((vt.astype(F32) * wk).astype(BF16), k, preferred_element_type=F32)
            un = jnp.dot(jnp.broadcast_to(wk, (8, L)).astype(BF16), k, preferred_element_type=F32)
            local[d][c] = (a, dsum, gmax, un)
        rows_of.append(per_dir)

    order = [list(range(nc)), list(range(ncc - 1, -1, -1)) + list(range(nc - 1, ncc - 1, -1))]
    for d in range(2):
        ct = jnp.zeros((MLSTM_V_DIM, MLSTM_QK_DIM), F32)
        n = jnp.zeros((8, MLSTM_QK_DIM), F32)
        m = jnp.zeros((1, 1), F32)
        for c in order[d]:
            irow, brow, blast = rows_of[c][d]
            a, dsum, gmax, un = local[d][c]
            q = qs_scr[c * L:(c + 1) * L, :]
            inter = brow + m
            m_t = jnp.maximum(inter, a)
            e_inter = jnp.exp(inter - m_t)
            e_intra = jnp.exp(a - m_t)
            qc = lax.dot_general(ct.astype(BF16), q, NT, preferred_element_type=F32)
            qn = lax.dot_general(n.astype(BF16), q, NT, preferred_element_type=F32)[0:1, :]
            numer = e_inter * qc + e_intra * h_scr[d, c]
            denom = e_inter * qn + e_intra * dsum
            h_scr[d, c] = numer * (1.0 / jnp.maximum(jnp.abs(denom), jnp.exp(-m_t)))
            m_new = jnp.maximum(blast + m, gmax)
            decay = jnp.exp(blast + m - m_new)
            gain = jnp.exp(gmax - m_new)
            ct = decay * ct + gain * u_scr[d, c]
            n = decay * n + gain * un
            m = m_new

    hn = hn_ref[...]
    for c in range(nc):
        is_ctx, rows = src_rows(c)
        if is_ctx and not with_ctx_out:
            continue
        o_ref, s_ref, y_ref = (oc_ref, sc_ref, yc_ref) if is_ctx else (ol_ref, sl_ref, yl_ref)
        hsum = h_scr[0, c] + h_scr[1, c]
        ms = jnp.mean(hsum * hsum, axis=0, keepdims=True)
        y = (hsum * lax.rsqrt(ms + EPS)).T * hn
        y = y * jax.nn.sigmoid(o_ref[rows, :].astype(F32))
        y_ref[rows, :] = (y * _silu(s_ref[rows, :].astype(F32))).astype(BF16)


def _mlstm_call(dims, p_lat, g_lat, p_ctx, g_ctx, conv_w, conv_b, i_bias, f_bias, head_norm, with_ctx_out):
    mh = dims.ml_heads
    seq, ctx = dims.seq, dims.ctx
    L = min(ctx, 256)
    qk, dv = MLSTM_QK_DIM, MLSTM_V_DIM
    lat = lambda width, off: pl.BlockSpec((seq, width), lambda b, h: (b, off // width + h))
    cx = lambda width, off: pl.BlockSpec((ctx, width), lambda b, h: (b, off // width + h))
    smem = pl.BlockSpec(memory_space=pltpu.SMEM)
    in_specs = [
        lat(qk, dims.col_qb), lat(qk, dims.col_kb), lat(dv, dims.col_vb), lat(LANES, 0),
        lat(dv, dims.col_ob), lat(dv, dims.col_gb),
        cx(qk, dims.col_qb), cx(qk, dims.col_kb), cx(dv, dims.col_vb), cx(LANES, 0),
        cx(dv, dims.col_ob), cx(dv, dims.col_gb),
        pl.BlockSpec((CONV_WIDTH, qk), lambda b, h: (0, h)),
        pl.BlockSpec((CONV_WIDTH, qk), lambda b, h: (0, mh + h)),
        pl.BlockSpec((1, qk), lambda b, h: (0, h)),
        pl.BlockSpec((1, qk), lambda b, h: (0, mh + h)),
        smem, smem,
        pl.BlockSpec((1, dv), lambda b, h: (0, h)),
    ]
    args = [p_lat, p_lat, p_lat, g_lat, p_lat, p_lat, p_ctx, p_ctx, p_ctx, g_ctx, p_ctx, p_ctx,
            conv_w, conv_w, conv_b, conv_b, i_bias, f_bias, head_norm]
    out_specs = [pl.BlockSpec((seq, dv), lambda b, h: (b, h))]
    out_shape = [jax.ShapeDtypeStruct((dims.n_lat, dims.ml_width), BF16)]
    if with_ctx_out:
        out_specs.append(pl.BlockSpec((ctx, dv), lambda b, h: (b, h)))
        out_shape.append(jax.ShapeDtypeStruct((dims.n_ctx, dims.ml_width), BF16))
    nc = (ctx + seq) // L
    scratch = [
        pltpu.VMEM((ctx + seq, qk), BF16), pltpu.VMEM((ctx + seq, qk), BF16),
        pltpu.VMEM((nc, dv, L), BF16),
        pltpu.VMEM((nc, 8, L), F32),
        pltpu.VMEM((2, nc, dv, L), F32),
        pltpu.VMEM((2, nc, dv, qk), F32),
    ]
    return pl.pallas_call(
        functools.partial(_mlstm_kernel, chunk=L, with_ctx_out=with_ctx_out),
        grid=(dims.batch, mh),
        in_specs=in_specs,
        out_specs=out_specs,
        out_shape=out_shape,
        scratch_shapes=scratch,
        compiler_params=_params("arbitrary", "arbitrary"),
        name="mlstm",
    )(*args)


def _outproj_kernel(ya_ref, ym_ref, w_ref, x_ref, gate_ref, nw_ref, o_ref, *, tm, rows_per_mod, fixed_mod_row):
    i = pl.program_id(0)
    d = o_ref.shape[1]
    ka = ya_ref.shape[1]
    nc = min(d, 512)
    ss = jnp.zeros((tm, 1), F32)
    for n in range(0, d, nc):
        acc = (jnp.dot(ya_ref[...], w_ref[0:ka, n:n + nc], preferred_element_type=F32)
               + jnp.dot(ym_ref[...], w_ref[ka:, n:n + nc], preferred_element_type=F32))
        ss = ss + jnp.sum(acc * acc, axis=1, keepdims=True)
        o_ref[:, n:n + nc] = acc
    row = fixed_mod_row if fixed_mod_row is not None else (i * tm) // rows_per_mod
    gate = gate_ref[pl.ds(row, 1), :]
    inv = lax.rsqrt(ss * (1.0 / d) + EPS)
    o_ref[...] = x_ref[...] + gate * (o_ref[...] * inv * nw_ref[...])


def _outproj_call(dims, y_att, y_ml, w_out, layer, x_flat, mod, nw, is_ctx):
    m, d = x_flat.shape
    tm = min(m, 512)
    kernel = functools.partial(_outproj_kernel, tm=tm, rows_per_mod=dims.seq,
                               fixed_mod_row=dims.batch if is_ctx else None)
    return pl.pallas_call(
        kernel,
        grid=(m // tm,),
        in_specs=[
            pl.BlockSpec((tm, y_att.shape[1]), lambda i: (i, 0)),
            pl.BlockSpec((tm, y_ml.shape[1]), lambda i: (i, 0)),
            pl.BlockSpec((None, d, d), lambda i: (layer, 0, 0)),
            pl.BlockSpec((tm, d), lambda i: (i, 0)),
            pl.BlockSpec((MOD_ROWS, d), lambda i: (0, 2)),
            pl.BlockSpec((1, d), lambda i: (0, 0)),
        ],
        out_specs=pl.BlockSpec((tm, d), lambda i: (i, 0)),
        out_shape=jax.ShapeDtypeStruct((m, d), F32),
        compiler_params=_params("arbitrary"),
        name="out_proj_ctx" if is_ctx else "out_proj",
    )(y_att, y_ml, w_out, x_flat, mod, nw)


def _rope_tables(seq):
    rows = seq // GRID_W
    row = jnp.repeat(jnp.arange(rows), GRID_W).astype(F32)
    col = jnp.tile(jnp.arange(GRID_W), rows).astype(F32)
    n_freq = ATT_QK_DIM // 4
    inv = ROPE_BASE ** (-jnp.arange(n_freq, dtype=F32) / n_freq)
    ang = jnp.concatenate([row[:, None] * inv, col[:, None] * inv], axis=-1)
    cos, sin = jnp.cos(ang), jnp.sin(ang)
    return jnp.tile(cos, (1, 4)), jnp.concatenate([-sin, -sin, sin, sin], axis=-1)


def _gate_weights(dims, w_in_l):
    d, mh = dims.d_model, dims.ml_heads
    wg = w_in_l[:, dims.main_cols:].reshape(d, 4, mh).transpose(0, 2, 1)
    return jnp.pad(wg, ((0, 0), (0, 0), (0, LANES - 4))).reshape(d, mh * LANES).astype(BF16)


def _forward(dims, x, c, ctx, c_ctx, w_ada, b_ada, norm_pre, norm_post, w_in, w_out,
             lam_q1, lam_k1, lam_q2, lam_k2, attn_subln, conv_w, conv_b, i_bias, f_bias, mlstm_norm):
    depth = w_ada.shape[0]
    d = dims.d_model
    x_flat = x.reshape(dims.n_lat, d)
    xc_flat = ctx.reshape(dims.n_ctx, d)
    c_all = jnp.concatenate(
        [c, c_ctx[None], jnp.zeros((MOD_ROWS - dims.batch - 1, d), F32)], axis=0)
    mod_all = _ada_call(c_all, w_ada, b_ada)
    w_main = _wprep_call(dims, w_in)
    w_out_b = _cast_call(w_out)
    cos_t, sin_t = _rope_tables(dims.seq)
    for l in range(depth):
        need_ctx = l < depth - 1
        mod = mod_all[l]
        w_gate = _gate_weights(dims, w_in[l])
        nw_pre = norm_pre[l].reshape(1, d)
        p_lat, g_lat = _inproj_call(dims, x_flat, mod, nw_pre, w_main, l, w_gate, cos_t, sin_t, False)
        p_ctx, g_ctx = _inproj_call(dims, xc_flat, mod, nw_pre, w_main, l, w_gate, cos_t, sin_t, True)
        lam_rows = jnp.stack([lam_q1[l], lam_k1[l], lam_q2[l], lam_k2[l]])
        subln = attn_subln[l].reshape(1, ATT_V_DIM)
        ya_lat = _attn_call(dims, p_lat, p_ctx, lam_rows, subln, l, False)
        ym = _mlstm_call(dims, p_lat, g_lat, p_ctx, g_ctx, conv_w[l], conv_b[l].reshape(1, -1),
                         i_bias[l], f_bias[l], mlstm_norm[l].reshape(1, -1), need_ctx)
        nw_post = norm_post[l].reshape(1, d)
        x_new = _outproj_call(dims, ya_lat, ym[0], w_out_b, l, x_flat, mod, nw_post, False)
        if need_ctx:
            ya_ctx = _attn_call(dims, p_lat, p_ctx, lam_rows, subln, l, True)
            xc_flat = _outproj_call(dims, ya_ctx, ym[1], w_out_b, l, xc_flat, mod, nw_post, True)
        x_flat = x_new
    return x_flat.reshape(x.shape)


def kernel(x, c, ctx, c_ctx, w_ada, b_ada, norm_pre, norm_post, w_in, w_out, lam_q1, lam_k1, lam_q2, lam_k2,
           attn_subln, conv_w, conv_b, i_bias, f_bias, mlstm_norm):
    dims = Dims(batch=x.shape[0], seq=x.shape[1], ctx=ctx.shape[1], d_model=x.shape[2])
    return _forward(dims, x, c, ctx, c_ctx, w_ada, b_ada, norm_pre, norm_post, w_in, w_out,
                    lam_q1, lam_k1, lam_q2, lam_k2, attn_subln, conv_w, conv_b, i_bias, f_bias, mlstm_norm)
```

```python
import functools
import math
from typing import NamedTuple

import numpy as np
import jax
import jax.numpy as jnp
from jax import lax
from jax.experimental import pallas as pl
from jax.experimental.pallas import tpu as pltpu

F32 = jnp.float32
BF16 = jnp.bfloat16

LANES = 128
ATT_V_DIM = 128
ATT_QK_DIM = 64
MLSTM_V_DIM = 256
MLSTM_QK_DIM = 128
CONV_WIDTH = 3
GATE_SOFTCAP = 15.0
ROPE_BASE = 10000.0
GRID_W = 64
EPS = 1e-6
MOD_ROWS = 16
VMEM_LIMIT = 56 * 1024 * 1024
NT = (((1,), (1,)), ((), ()))


class Dims(NamedTuple):
    batch: int
    seq: int
    ctx: int
    d_model: int

    @property
    def att_width(self):
        return self.d_model // 2

    @property
    def att_heads(self):
        return self.att_width // ATT_V_DIM

    @property
    def ml_width(self):
        return self.d_model - self.att_width

    @property
    def ml_heads(self):
        return self.ml_width // MLSTM_V_DIM

    @property
    def n_lat(self):
        return self.batch * self.seq

    @property
    def n_ctx(self):
        return self.batch * self.ctx

    @property
    def col_qa(self):
        return 0

    @property
    def col_ka(self):
        return self.att_width

    @property
    def col_va(self):
        return 2 * self.att_width

    @property
    def col_ga(self):
        return 3 * self.att_width

    @property
    def col_qb(self):
        return 4 * self.att_width

    @property
    def col_kb(self):
        return self.col_qb + self.ml_heads * MLSTM_QK_DIM

    @property
    def col_vb(self):
        return self.col_kb + self.ml_heads * MLSTM_QK_DIM

    @property
    def col_ob(self):
        return self.col_vb + self.ml_width

    @property
    def col_gb(self):
        return self.col_ob + self.ml_width

    @property
    def main_cols(self):
        return self.col_gb + self.ml_width


def _params(*sem):
    return pltpu.CompilerParams(dimension_semantics=sem, vmem_limit_bytes=VMEM_LIMIT)


def _silu(x):
    return x * jax.nn.sigmoid(x)


def _wprep_kernel(w_ref, wg_ref, perm_ref, gperm_ref, o_ref, og_ref, *, qk_tiles, q_tiles, q_scale, n_gate):
    j = pl.program_id(2)

    @pl.when(j < qk_tiles)
    def _():
        x = (w_ref[0] * jnp.where(j < q_tiles, q_scale, 1.0)).astype(BF16)
        for c in range(0, x.shape[1], LANES):
            o_ref[0, :, c:c + LANES] = jnp.dot(
                x[:, c:c + LANES], perm_ref[...], preferred_element_type=F32).astype(BF16)

    @pl.when(j >= qk_tiles)
    def _():
        o_ref[0] = w_ref[0].astype(BF16)

    @pl.when(j == 0)
    def _():
        lane = lax.broadcasted_iota(jnp.int32, (1, LANES), 1)
        g = jnp.where(lane < n_gate, wg_ref[0], 0.0).astype(BF16)
        og_ref[0] = jnp.dot(g, gperm_ref[...], preferred_element_type=F32).astype(BF16)


def _rope_lane_permutation():
    perm = np.zeros((LANES, LANES), np.float32)
    for c in range(2):
        for i in range(ATT_QK_DIM // 2):
            for par in range(2):
                perm[c * ATT_QK_DIM + 2 * i + par, par * ATT_QK_DIM + c * (ATT_QK_DIM // 2) + i] = 1.0
    return jnp.asarray(perm, BF16)


def _gate_lane_placement(dims):
    mh = dims.ml_heads
    place = np.zeros((LANES, mh * LANES), np.float32)
    for kind in range(4):
        for h in range(mh):
            place[kind * mh + h, h * LANES + kind] = 1.0
    return jnp.asarray(place, BF16)


def _wprep_call(dims, w_in):
    depth, d, n_all = w_in.shape
    mh = dims.ml_heads
    tn = min(dims.att_width, 1024)
    tr = min(d, 512)
    n_gate = n_all - dims.main_cols
    assert dims.att_width % tn == 0 and dims.main_cols % tn == 0
    assert dims.main_cols % LANES == 0 and n_gate == 4 * mh <= LANES
    kernel = functools.partial(_wprep_kernel, qk_tiles=2 * dims.att_width // tn, q_tiles=dims.att_width // tn,
                               q_scale=ATT_QK_DIM ** -0.5 * math.log2(math.e), n_gate=n_gate)
    return pl.pallas_call(
        kernel,
        grid=(depth, d // tr, dims.main_cols // tn),
        in_specs=[
            pl.BlockSpec((1, tr, tn), lambda l, i, j: (l, i, j)),
            pl.BlockSpec((1, tr, LANES), lambda l, i, j: (l, i, dims.main_cols // LANES)),
            pl.BlockSpec((LANES, LANES), lambda l, i, j: (0, 0)),
            pl.BlockSpec((LANES, mh * LANES), lambda l, i, j: (0, 0)),
        ],
        out_specs=[
            pl.BlockSpec((1, tr, tn), lambda l, i, j: (l, i, j)),
            pl.BlockSpec((1, tr, mh * LANES), lambda l, i, j: (l, i, 0)),
        ],
        out_shape=[
            jax.ShapeDtypeStruct((depth, d, dims.main_cols), BF16),
            jax.ShapeDtypeStruct((depth, d, mh * LANES), BF16),
        ],
        compiler_params=_params("arbitrary", "arbitrary", "arbitrary"),
        name="w_prep",
    )(w_in, w_in, _rope_lane_permutation(), _gate_lane_placement(dims))


def _cast_kernel(w_ref, o_ref):
    o_ref[...] = w_ref[...].astype(BF16)


def _cast_call(w):
    depth, r, c = w.shape
    tr = min(r, 512)
    return pl.pallas_call(
        _cast_kernel,
        grid=(depth, r // tr),
        in_specs=[pl.BlockSpec((1, tr, c), lambda l, i: (l, i, 0))],
        out_specs=pl.BlockSpec((1, tr, c), lambda l, i: (l, i, 0)),
        out_shape=jax.ShapeDtypeStruct(w.shape, BF16),
        compiler_params=_params("arbitrary", "arbitrary"),
        name="w_cast",
    )(w)


def _ada_kernel(c_ref, w_ref, b_ref, o_ref):
    s = _silu(c_ref[...]).astype(BF16)
    w = w_ref[0].astype(BF16)
    o_ref[0] = jnp.dot(s, w, preferred_element_type=F32) + b_ref[0]


def _ada_call(c_all, w_ada, b_ada):
    depth, d, n = w_ada.shape
    tn = min(n, 1024)
    return pl.pallas_call(
        _ada_kernel,
        grid=(depth, n // tn),
        in_specs=[
            pl.BlockSpec((MOD_ROWS, d), lambda l, j: (0, 0)),
            pl.BlockSpec((1, d, tn), lambda l, j: (l, 0, j)),
            pl.BlockSpec((1, 1, tn), lambda l, j: (l, 0, j)),
        ],
        out_specs=pl.BlockSpec((1, MOD_ROWS, tn), lambda l, j: (l, 0, j)),
        out_shape=jax.ShapeDtypeStruct((depth, MOD_ROWS, n), F32),
        compiler_params=_params("arbitrary", "arbitrary"),
        name="ada",
    )(c_all, w_ada, b_ada.reshape(depth, 1, n))


def _inproj_kernel(x_ref, shift_ref, scale_ref, nw_ref, w_ref, wg_ref, cos_ref, sin_ref, p_ref, g_ref, h_scr,
                   *, tm, tn, rows_per_mod, fixed_mod_row, rope_tiles):
    i = pl.program_id(0)
    j = pl.program_id(1)
    rc = min(tm, 256)

    @pl.when(j == 0)
    def _():
        row = fixed_mod_row if fixed_mod_row is not None else (i * tm) // rows_per_mod
        shift = shift_ref[pl.ds(row, 1), :]
        scale1 = 1.0 + scale_ref[pl.ds(row, 1), :]
        nw = nw_ref[...]
        for r in range(0, tm, rc):
            xf = x_ref[r:r + rc, :]
            ms = jnp.mean(xf * xf, axis=-1, keepdims=True)
            y = xf * lax.rsqrt(ms + EPS) * nw
            h_scr[r:r + rc, :] = (y * scale1 + shift).astype(BF16)
        g_ref[...] = jnp.dot(h_scr[...], wg_ref[...], preferred_element_type=F32)

    def project(rope):
        for n in range(0, tn, 2 * LANES):
            acc = jnp.dot(h_scr[...], w_ref[:, n:n + 2 * LANES], preferred_element_type=F32)
            if rope:
                for hh in range(2):
                    a = acc[:, hh * LANES:(hh + 1) * LANES]
                    r = a * cos_ref[...] + pltpu.roll(a, LANES // 2, axis=1) * sin_ref[...]
                    p_ref[:, n + hh * LANES:n + (hh + 1) * LANES] = r.astype(BF16)
            else:
                p_ref[:, n:n + 2 * LANES] = acc.astype(BF16)

    if rope_tiles:
        pl.when(j < rope_tiles)(lambda: project(True))
        pl.when(j >= rope_tiles)(lambda: project(False))
    else:
        project(False)


def _inproj_call(dims, x_flat, mod, nw, w_main, layer, w_gate, cos_t, sin_t, is_ctx):
    m, d = x_flat.shape
    n_main = w_main.shape[2]
    n_gate = w_gate.shape[2]
    tm = min(m if is_ctx else dims.seq, 1024)
    tn = min(n_main, 1024)
    assert (2 * dims.att_width) % tn == 0
    kernel = functools.partial(
        _inproj_kernel, tm=tm, tn=tn, rows_per_mod=dims.seq,
        fixed_mod_row=dims.batch if is_ctx else None,
        rope_tiles=0 if is_ctx else (2 * dims.att_width) // tn)
    tiles_per_seq = max(dims.seq // tm, 1)
    return pl.pallas_call(
        kernel,
        grid=(m // tm, n_main // tn),
        in_specs=[
            pl.BlockSpec((tm, d), lambda i, j: (i, 0)),
            pl.BlockSpec((MOD_ROWS, d), lambda i, j: (0, 0)),
            pl.BlockSpec((MOD_ROWS, d), lambda i, j: (0, 1)),
            pl.BlockSpec((1, d), lambda i, j: (0, 0)),
            pl.BlockSpec((None, d, tn), lambda i, j: (layer, 0, j)),
            pl.BlockSpec((None, d, n_gate), lambda i, j: (layer, 0, 0)),
            pl.BlockSpec((tm, LANES), lambda i, j: (i % tiles_per_seq, 0)),
            pl.BlockSpec((tm, LANES), lambda i, j: (i % tiles_per_seq, 0)),
        ],
        out_specs=[
            pl.BlockSpec((tm, tn), lambda i, j: (i, j)),
            pl.BlockSpec((tm, n_gate), lambda i, j: (i, 0)),
        ],
        out_shape=[
            jax.ShapeDtypeStruct((m, n_main), BF16),
            jax.ShapeDtypeStruct((m, n_gate), F32),
        ],
        scratch_shapes=[pltpu.VMEM((tm, d), BF16)],
        compiler_params=_params("arbitrary", "arbitrary"),
        name="in_proj_ctx" if is_ctx else "in_proj",
    )(x_flat, mod, mod, nw, w_main, w_gate, cos_t, sin_t)


def _lambda(lam_ref, lam_init):
    lp = lam_ref[...]
    return (jnp.exp(jnp.sum(lp[0:1] * lp[1:2], axis=1, keepdims=True))
            - jnp.exp(jnp.sum(lp[2:3] * lp[3:4], axis=1, keepdims=True)) + lam_init)


def _first_sub_head_lanes():
    lane = lax.broadcasted_iota(jnp.int32, (1, LANES), 1)
    return (lane % (LANES // 2)) < (LANES // 4)


def _attn_finish(o_t, lam_init, sub_ref, g):
    ms = jnp.mean(o_t * o_t, axis=0, keepdims=True)
    y = (o_t * lax.rsqrt(ms + EPS)).T
    y = y * sub_ref[...] * (1.0 - lam_init)
    return (y * _silu(g.astype(F32))).astype(BF16)


def _attn_lat_kernel(q_ref, kc_ref, vc_ref, kl_ref, vl_ref, g_ref, lam_ref, sub_ref, o_ref,
                     vt_scr, s_scr, *, lam_init, tq, kc):
    ctx, seq = kc_ref.shape[0], kl_ref.shape[0]
    n_tiles = seq // tq
    n_kc = (ctx + seq) // kc

    for r in range(0, ctx, kc):
        vt_scr[:, r:r + kc] = vc_ref[r:r + kc, :].astype(F32).T.astype(BF16)
    for r in range(0, seq, kc):
        vt_scr[:, ctx + r:ctx + r + kc] = vl_ref[r:r + kc, :].astype(F32).T.astype(BF16)

    lam = _lambda(lam_ref, lam_init)
    first = _first_sub_head_lanes()

    def unit_queries(u):
        t, sub = divmod(u, 2)
        q = q_ref[t * tq:(t + 1) * tq, :]
        return jnp.where(first if sub == 0 else jnp.logical_not(first), q, jnp.zeros_like(q))

    def scores(qs, slot):
        s_c = lax.dot_general(kc_ref[...], qs, NT, preferred_element_type=F32)
        s_l = lax.dot_general(kl_ref[...], qs, NT, preferred_element_type=F32)
        s_scr[slot, 0:ctx] = s_c
        s_scr[slot, ctx:] = s_l
        return jnp.maximum(jnp.max(s_c, axis=0, keepdims=True), jnp.max(s_l, axis=0, keepdims=True))

    def softmax_values(slot, m):
        den = jnp.zeros((1, tq), F32)
        o_t = jnp.zeros((ATT_V_DIM, tq), F32)
        for i in range(n_kc):
            p = jnp.exp2(s_scr[slot, i * kc:(i + 1) * kc, :] - m)
            den = den + jnp.sum(p, axis=0, keepdims=True)
            o_t = o_t + jnp.dot(vt_scr[:, i * kc:(i + 1) * kc], p.astype(BF16), preferred_element_type=F32)
        return o_t * (1.0 / den)

    n_units = 2 * n_tiles
    m = scores(unit_queries(0), 0)
    o_first = None
    for u in range(n_units):
        m_next = scores(unit_queries(u + 1), (u + 1) % 2) if u + 1 < n_units else None
        o_t = softmax_values(u % 2, m)
        m = m_next
        if u % 2 == 0:
            o_first = o_t
        else:
            rows = slice((u // 2) * tq, (u // 2 + 1) * tq)
            o_ref[rows, :] = _attn_finish(o_first - lam * o_t, lam_init, sub_ref, g_ref[rows, :])


def _attn_ctx_kernel(q_ref, kc_ref, vc_ref, g_ref, lam_ref, sub_ref, o_ref, *, lam_init):
    lam = _lambda(lam_ref, lam_init)
    first = _first_sub_head_lanes()
    for h in range(q_ref.shape[1] // LANES):
        cols = slice(h * LANES, (h + 1) * LANES)
        q = q_ref[:, cols]
        k = kc_ref[:, cols]
        vt = vc_ref[:, cols].astype(F32).T.astype(BF16)
        outs = []
        for sub in range(2):
            qs = jnp.where(first if sub == 0 else jnp.logical_not(first), q, jnp.zeros_like(q))
            s = lax.dot_general(k, qs, NT, preferred_element_type=F32)
            p = jnp.exp2(s - jnp.max(s, axis=0, keepdims=True))
            den = jnp.sum(p, axis=0, keepdims=True)
            outs.append(jnp.dot(vt, p.astype(BF16), preferred_element_type=F32) * (1.0 / den))
        o_ref[:, cols] = _attn_finish(outs[0] - lam * outs[1], lam_init, sub_ref, g_ref[:, cols])


def _attn_call(dims, p_lat, p_ctx, lam_rows, subln, layer_idx, is_ctx):
    lam_init = 0.8 - 0.6 * math.exp(-0.3 * layer_idx)
    ah = dims.att_heads
    seq, ctx = dims.seq, dims.ctx
    cb = lambda off: off // LANES
    lat = lambda off: pl.BlockSpec((seq, LANES), lambda b, h: (b, cb(off) + h))
    cx = lambda off: pl.BlockSpec((ctx, LANES), lambda b, h: (b, cb(off) + h))
    small = [pl.BlockSpec((4, ATT_QK_DIM), lambda b, h: (0, 0)),
             pl.BlockSpec((1, ATT_V_DIM), lambda b, h: (0, 0))]
    if is_ctx:
        aw = dims.att_width
        wide = lambda off: pl.BlockSpec((ctx, aw), lambda b: (b, off // aw))
        return pl.pallas_call(
            functools.partial(_attn_ctx_kernel, lam_init=lam_init),
            grid=(dims.batch,),
            in_specs=[wide(dims.col_qa), wide(dims.col_ka), wide(dims.col_va), wide(dims.col_ga),
                      pl.BlockSpec((4, ATT_QK_DIM), lambda b: (0, 0)),
                      pl.BlockSpec((1, ATT_V_DIM), lambda b: (0, 0))],
            out_specs=pl.BlockSpec((ctx, aw), lambda b: (b, 0)),
            out_shape=jax.ShapeDtypeStruct((dims.n_ctx, aw), BF16),
            compiler_params=_params("arbitrary"),
            name="attn_ctx",
        )(p_ctx, p_ctx, p_ctx, p_ctx, lam_rows, subln)
    tq = min(seq, 512)
    kc = min(ctx, 256)
    return pl.pallas_call(
        functools.partial(_attn_lat_kernel, lam_init=lam_init, tq=tq, kc=kc),
        grid=(dims.batch, ah),
        in_specs=[lat(dims.col_qa), cx(dims.col_ka), cx(dims.col_va), lat(dims.col_ka), lat(dims.col_va),
                  lat(dims.col_ga)] + small,
        out_specs=pl.BlockSpec((seq, LANES), lambda b, h: (b, h)),
        out_shape=jax.ShapeDtypeStruct((dims.n_lat, dims.att_width), BF16),
        scratch_shapes=[
            pltpu.VMEM((ATT_V_DIM, ctx + seq), BF16),
            pltpu.VMEM((2, ctx + seq, tq), F32),
        ],
        compiler_params=_params("arbitrary", "arbitrary"),
        name="attn",
    )(p_lat, p_ctx, p_ctx, p_lat, p_lat, p_lat, lam_rows, subln)


def _split3(x):
    hi = x.astype(BF16)
    r1 = x - hi.astype(F32)
    mid = r1.astype(BF16)
    lo = (r1 - mid.astype(F32)).astype(BF16)
    return hi, mid, lo


def _conv_silu(src_ref, w, b, scale, dst_ref):
    x = src_ref[...].astype(F32)
    t = x.shape[0]
    row = lax.broadcasted_iota(jnp.int32, (t, 1), 0)
    prev = jnp.where(row == 0, 0.0, pltpu.roll(x, 1, axis=0))
    nxt = jnp.where(row == t - 1, 0.0, pltpu.roll(x, t - 1, axis=0))
    y = b + w[0:1] * prev
    y = y + w[1:2] * x
    y = y + w[2:3] * nxt
    y = _silu(y)
    if scale is not None:
        y = y * scale
    dst_ref[...] = y.astype(BF16)


def _mlstm_kernel(*refs, chunk, with_ctx_out):
    (ql_ref, kl_ref, vl_ref, gl_ref, ol_ref, sl_ref,
     qc_ref, kc_ref, vc_ref, gc_ref, oc_ref, sc_ref,
     cwq_ref, cwk_ref, cbq_ref, cbk_ref, ib_ref, fb_ref, hn_ref) = refs[:19]
    refs = refs[19:]
    if with_ctx_out:
        yl_ref, yc_ref = refs[:2]
        refs = refs[2:]
    else:
        yl_ref, yc_ref = refs[0], None
        refs = refs[1:]
    qs_scr, ks_scr, vt_scr, gr_scr, h_scr, u_scr = refs
    L = chunk
    ctx, seq = qc_ref.shape[0], ql_ref.shape[0]
    ncc, nc = ctx // L, (ctx + seq) // L
    hd = pl.program_id(1)

    k_scale = MLSTM_QK_DIM ** -0.5
    _conv_silu(qc_ref, cwq_ref[...], cbq_ref[...], None, qs_scr.at[0:ctx])
    _conv_silu(kc_ref, cwk_ref[...], cbk_ref[...], k_scale, ks_scr.at[0:ctx])
    _conv_silu(ql_ref, cwq_ref[...], cbq_ref[...], None, qs_scr.at[ctx:])
    _conv_silu(kl_ref, cwk_ref[...], cbk_ref[...], k_scale, ks_scr.at[ctx:])

    def src_rows(c):
        return (True, slice(c * L, (c + 1) * L)) if c < ncc else (False, slice((c - ncc) * L, (c - ncc + 1) * L))

    for c in range(nc):
        is_ctx, rows = src_rows(c)
        vt_scr[c] = (vc_ref if is_ctx else vl_ref)[rows, :].astype(F32).T.astype(BF16)
        gr_scr[c] = (gc_ref if is_ctx else gl_ref)[rows, :].T[0:8, :]

    row = lax.broadcasted_iota(jnp.int32, (1, 8, 1), 1)
    bias = jnp.where(row == 0, ib_ref[0, hd],
                     jnp.where(row == 1, ib_ref[1, hd],
                               jnp.where(row == 2, fb_ref[0, hd],
                                         jnp.where(row == 3, fb_ref[1, hd], 0.0))))
    capped = GATE_SOFTCAP * jnp.tanh((gr_scr[...] + bias) * (1.0 / GATE_SOFTCAP))
    lsig = jnp.minimum(capped, 0.0) - jnp.log1p(jnp.exp(-jnp.abs(capped)))
    lf = jnp.where((row >= 2) & (row < 4), lsig, 0.0).reshape(nc * 8, L)
    ui = lax.broadcasted_iota(jnp.int32, (L, L), 0)
    ti = lax.broadcasted_iota(jnp.int32, (L, L), 1)
    tri_t = (ui <= ti).astype(BF16)
    hi, mid, lo = _split3(lf)
    prefix = (jnp.dot(hi, tri_t, preferred_element_type=F32) + jnp.dot(mid, tri_t, preferred_element_type=F32)
              + jnp.dot(lo, tri_t, preferred_element_type=F32))
    tot = prefix[:, L - 1:L]
    suffix = tot - prefix + lf
    row2 = lax.broadcasted_iota(jnp.int32, (nc * 8, 1), 0) % 8
    gates = jnp.where(row2 < 2, capped.reshape(nc * 8, L), jnp.where(row2 == 2, prefix, suffix))

    mask_f = ui <= ti
    mask_b = ui >= ti
    zeros_pad = jnp.zeros((LANES - 8, L), F32)

    rows_of = []
    local = [[None] * nc for _ in range(2)]
    for c in range(nc):
        g8 = gates[c * 8:(c + 1) * 8, :]
        t8 = tot[c * 8:(c + 1) * 8, :]
        dcols = jnp.concatenate([g8 - pltpu.roll(g8, 6, axis=0), zeros_pad], axis=0).T
        q = qs_scr[c * L:(c + 1) * L, :]
        k = ks_scr[c * L:(c + 1) * L, :]
        vt = vt_scr[c]
        qk = lax.dot_general(k, q, NT, preferred_element_type=F32)
        per_dir = []
        for d in range(2):
            irow, brow, blast = g8[d:d + 1, :], g8[2 + d:3 + d, :], t8[2 + d:3 + d, :]
            per_dir.append((irow, brow, blast))
            logw = jnp.where(mask_f if d == 0 else mask_b, brow + dcols[:, d:d + 1], -jnp.inf)
            a = jnp.max(logw, axis=0, keepdims=True)
            s = qk * jnp.exp(logw - a)
            h_scr[d, c] = jnp.dot(vt, s.astype(BF16), preferred_element_type=F32)
            dsum = jnp.sum(s, axis=0, keepdims=True)
            grow = blast - brow + irow
            gmax = jnp.max(grow, axis=1, keepdims=True)
            wk = jnp.exp(grow - gmax)
            u_scr[d, c] = jnp.dot((vt.astype(F32) * wk).astype(BF16), k, preferred_element_type=F32)
            un = jnp.dot(jnp.broadcast_to(wk, (8, L)).astype(BF16), k, preferred_element_type=F32)
            local[d][c] = (a, dsum, gmax, un)
        rows_of.append(per_dir)

    order = [list(range(nc)), list(range(ncc - 1, -1, -1)) + list(range(nc - 1, ncc - 1, -1))]
    for d in range(2):
        ct = jnp.zeros((MLSTM_V_DIM, MLSTM_QK_DIM), F32)
        n = jnp.zeros((8, MLSTM_QK_DIM), F32)
        m = jnp.zeros((1, 1), F32)
        for c in order[d]:
            irow, brow, blast = rows_of[c][d]
            a, dsum, gmax, un = local[d][c]
            q = qs_scr[c * L:(c + 1) * L, :]
            inter = brow + m
            m_t = jnp.maximum(inter, a)
            e_inter = jnp.exp(inter - m_t)
            e_intra = jnp.exp(a - m_t)
            qc = lax.dot_general(ct.astype(BF16), q, NT, preferred_element_type=F32)
            qn = lax.dot_general(n.astype(BF16), q, NT, preferred_element_type=F32)[0:1, :]
            numer = e_inter * qc + e_intra * h_scr[d, c]
            denom = e_inter * qn + e_intra * dsum
            h_scr[d, c] = numer * (1.0 / jnp.maximum(jnp.abs(denom), jnp.exp(-m_t)))
            m_new = jnp.maximum(blast + m, gmax)
            decay = jnp.exp(blast + m - m_new)
            gain = jnp.exp(gmax - m_new)
            ct = decay * ct + gain * u_scr[d, c]
            n = decay * n + gain * un
            m = m_new

    hn = hn_ref[...]
    for c in range(nc):
        is_ctx, rows = src_rows(c)
        if is_ctx and not with_ctx_out:
            continue
        o_ref, s_ref, y_ref = (oc_ref, sc_ref, yc_ref) if is_ctx else (ol_ref, sl_ref, yl_ref)
        hsum = h_scr[0, c] + h_scr[1, c]
        ms = jnp.mean(hsum * hsum, axis=0, keepdims=True)
        y = (hsum * lax.rsqrt(ms + EPS)).T * hn
        y = y * jax.nn.sigmoid(o_ref[rows, :].astype(F32))
        y_ref[rows, :] = (y * _silu(s_ref[rows, :].astype(F32))).astype(BF16)


def _mlstm_call(dims, p_lat, g_lat, p_ctx, g_ctx, conv_w, conv_b, i_bias, f_bias, head_norm, with_ctx_out):
    mh = dims.ml_heads
    seq, ctx = dims.seq, dims.ctx
    L = min(ctx, 256)
    qk, dv = MLSTM_QK_DIM, MLSTM_V_DIM
    lat = lambda width, off: pl.BlockSpec((seq, width), lambda b, h: (b, off // width + h))
    cx = lambda width, off: pl.BlockSpec((ctx, width), lambda b, h: (b, off // width + h))
    smem = pl.BlockSpec(memory_space=pltpu.SMEM)
    in_specs = [
        lat(qk, dims.col_qb), lat(qk, dims.col_kb), lat(dv, dims.col_vb), lat(LANES, 0),
        lat(dv, dims.col_ob), lat(dv, dims.col_gb),
        cx(qk, dims.col_qb), cx(qk, dims.col_kb), cx(dv, dims.col_vb), cx(LANES, 0),
        cx(dv, dims.col_ob), cx(dv, dims.col_gb),
        pl.BlockSpec((CONV_WIDTH, qk), lambda b, h: (0, h)),
        pl.BlockSpec((CONV_WIDTH, qk), lambda b, h: (0, mh + h)),
        pl.BlockSpec((1, qk), lambda b, h: (0, h)),
        pl.BlockSpec((1, qk), lambda b, h: (0, mh + h)),
        smem, smem,
        pl.BlockSpec((1, dv), lambda b, h: (0, h)),
    ]
    args = [p_lat, p_lat, p_lat, g_lat, p_lat, p_lat, p_ctx, p_ctx, p_ctx, g_ctx, p_ctx, p_ctx,
            conv_w, conv_w, conv_b, conv_b, i_bias, f_bias, head_norm]
    out_specs = [pl.BlockSpec((seq, dv), lambda b, h: (b, h))]
    out_shape = [jax.ShapeDtypeStruct((dims.n_lat, dims.ml_width), BF16)]
    if with_ctx_out:
        out_specs.append(pl.BlockSpec((ctx, dv), lambda b, h: (b, h)))
        out_shape.append(jax.ShapeDtypeStruct((dims.n_ctx, dims.ml_width), BF16))
    nc = (ctx + seq) // L
    scratch = [
        pltpu.VMEM((ctx + seq, qk), BF16), pltpu.VMEM((ctx + seq, qk), BF16),
        pltpu.VMEM((nc, dv, L), BF16),
        pltpu.VMEM((nc, 8, L), F32),
        pltpu.VMEM((2, nc, dv, L), F32),
        pltpu.VMEM((2, nc, dv, qk), F32),
    ]
    return pl.pallas_call(
        functools.partial(_mlstm_kernel, chunk=L, with_ctx_out=with_ctx_out),
        grid=(dims.batch, mh),
        in_specs=in_specs,
        out_specs=out_specs,
        out_shape=out_shape,
        scratch_shapes=scratch,
        compiler_params=_params("arbitrary", "arbitrary"),
        name="mlstm",
    )(*args)


def _outproj_kernel(ya_ref, ym_ref, w_ref, x_ref, gate_ref, nw_ref, o_ref, *, tm, rows_per_mod, fixed_mod_row):
    i = pl.program_id(0)
    d = o_ref.shape[1]
    ka = ya_ref.shape[1]
    nc = min(d, 512)
    ss = jnp.zeros((tm, 1), F32)
    for n in range(0, d, nc):
        acc = (jnp.dot(ya_ref[...], w_ref[0:ka, n:n + nc], preferred_element_type=F32)
               + jnp.dot(ym_ref[...], w_ref[ka:, n:n + nc], preferred_element_type=F32))
        ss = ss + jnp.sum(acc * acc, axis=1, keepdims=True)
        o_ref[:, n:n + nc] = acc
    row = fixed_mod_row if fixed_mod_row is not None else (i * tm) // rows_per_mod
    gate = gate_ref[pl.ds(row, 1), :]
    inv = lax.rsqrt(ss * (1.0 / d) + EPS)
    o_ref[...] = x_ref[...] + gate * (o_ref[...] * inv * nw_ref[...])


def _outproj_call(dims, y_att, y_ml, w_out, layer, x_flat, mod, nw, is_ctx):
    m, d = x_flat.shape
    tm = min(m, 512)
    kernel = functools.partial(_outproj_kernel, tm=tm, rows_per_mod=dims.seq,
                               fixed_mod_row=dims.batch if is_ctx else None)
    return pl.pallas_call(
        kernel,
        grid=(m // tm,),
        in_specs=[
            pl.BlockSpec((tm, y_att.shape[1]), lambda i: (i, 0)),
            pl.BlockSpec((tm, y_ml.shape[1]), lambda i: (i, 0)),
            pl.BlockSpec((None, d, d), lambda i: (layer, 0, 0)),
            pl.BlockSpec((tm, d), lambda i: (i, 0)),
            pl.BlockSpec((MOD_ROWS, d), lambda i: (0, 2)),
            pl.BlockSpec((1, d), lambda i: (0, 0)),
        ],
        out_specs=pl.BlockSpec((tm, d), lambda i: (i, 0)),
        out_shape=jax.ShapeDtypeStruct((m, d), F32),
        compiler_params=_params("arbitrary"),
        name="out_proj_ctx" if is_ctx else "out_proj",
    )(y_att, y_ml, w_out, x_flat, mod, nw)


def _rope_tables(seq):
    rows = seq // GRID_W
    row = jnp.repeat(jnp.arange(rows), GRID_W).astype(F32)
    col = jnp.tile(jnp.arange(GRID_W), rows).astype(F32)
    n_freq = ATT_QK_DIM // 4
    inv = ROPE_BASE ** (-jnp.arange(n_freq, dtype=F32) / n_freq)
    ang = jnp.concatenate([row[:, None] * inv, col[:, None] * inv], axis=-1)
    cos, sin = jnp.cos(ang), jnp.sin(ang)
    return jnp.tile(cos, (1, 4)), jnp.concatenate([-sin, -sin, sin, sin], axis=-1)


def _forward(dims, x, c, ctx, c_ctx, w_ada, b_ada, norm_pre, norm_post, w_in, w_out,
             lam_q1, lam_k1, lam_q2, lam_k2, attn_subln, conv_w, conv_b, i_bias, f_bias, mlstm_norm):
    depth = w_ada.shape[0]
    d = dims.d_model
    x_flat = x.reshape(dims.n_lat, d)
    xc_flat = ctx.reshape(dims.n_ctx, d)
    c_all = jnp.concatenate(
        [c, c_ctx[None], jnp.zeros((MOD_ROWS - dims.batch - 1, d), F32)], axis=0)
    mod_all = _ada_call(c_all, w_ada, b_ada)
    w_main, w_gate = _wprep_call(dims, w_in)
    w_out_b = _cast_call(w_out)
    cos_t, sin_t = _rope_tables(dims.seq)
    for l in range(depth):
        need_ctx = l < depth - 1
        mod = mod_all[l]
        nw_pre = norm_pre[l].reshape(1, d)
        p_lat, g_lat = _inproj_call(dims, x_flat, mod, nw_pre, w_main, l, w_gate, cos_t, sin_t, False)
        p_ctx, g_ctx = _inproj_call(dims, xc_flat, mod, nw_pre, w_main, l, w_gate, cos_t, sin_t, True)
        lam_rows = jnp.stack([lam_q1[l], lam_k1[l], lam_q2[l], lam_k2[l]])
        subln = attn_subln[l].reshape(1, ATT_V_DIM)
        ya_lat = _attn_call(dims, p_lat, p_ctx, lam_rows, subln, l, False)
        ym = _mlstm_call(dims, p_lat, g_lat, p_ctx, g_ctx, conv_w[l], conv_b[l].reshape(1, -1),
                         i_bias[l], f_bias[l], mlstm_norm[l].reshape(1, -1), need_ctx)
        nw_post = norm_post[l].reshape(1, d)
        x_new = _outproj_call(dims, ya_lat, ym[0], w_out_b, l, x_flat, mod, nw_post, False)
        if need_ctx:
            ya_ctx = _attn_call(dims, p_lat, p_ctx, lam_rows, subln, l, True)
            xc_flat = _outproj_call(dims, ya_ctx, ym[1], w_out_b, l, xc_flat, mod, nw_post, True)
        x_flat = x_new
    return x_flat.reshape(x.shape)


def kernel(x, c, ctx, c_ctx, w_ada, b_ada, norm_pre, norm_post, w_in, w_out, lam_q1, lam_k1, lam_q2, lam_k2,
           attn_subln, conv_w, conv_b, i_bias, f_bias, mlstm_norm):
    dims = Dims(batch=x.shape[0], seq=x.shape[1], ctx=ctx.shape[1], d_model=x.shape[2])
    return _forward(dims, x, c, ctx, c_ctx, w_ada, b_ada, norm_pre, norm_post, w_in, w_out,
                    lam_q1, lam_k1, lam_q2, lam_k2, attn_subln, conv_w, conv_b, i_bias, f_bias, mlstm_norm)
```

```python
import functools
import math
from typing import NamedTuple

import numpy as np
import jax
import jax.numpy as jnp
from jax import lax
from jax.experimental import pallas as pl
from jax.experimental.pallas import tpu as pltpu

F32 = jnp.float32
BF16 = jnp.bfloat16

LANES = 128
ATT_V_DIM = 128
ATT_QK_DIM = 64
MLSTM_V_DIM = 256
MLSTM_QK_DIM = 128
CONV_WIDTH = 3
GATE_SOFTCAP = 15.0
ROPE_BASE = 10000.0
GRID_W = 64
EPS = 1e-6
MOD_ROWS = 16
VMEM_LIMIT = 56 * 1024 * 1024
NT = (((1,), (1,)), ((), ()))


class Dims(NamedTuple):
    batch: int
    seq: int
    ctx: int
    d_model: int

    @property
    def att_width(self):
        return self.d_model // 2

    @property
    def att_heads(self):
        return self.att_width // ATT_V_DIM

    @property
    def ml_width(self):
        return self.d_model - self.att_width

    @property
    def ml_heads(self):
        return self.ml_width // MLSTM_V_DIM

    @property
    def n_lat(self):
        return self.batch * self.seq

    @property
    def n_ctx(self):
        return self.batch * self.ctx

    @property
    def col_qa(self):
        return 0

    @property
    def col_ka(self):
        return self.att_width

    @property
    def col_va(self):
        return 2 * self.att_width

    @property
    def col_ga(self):
        return 3 * self.att_width

    @property
    def col_qb(self):
        return 4 * self.att_width

    @property
    def col_kb(self):
        return self.col_qb + self.ml_heads * MLSTM_QK_DIM

    @property
    def col_vb(self):
        return self.col_kb + self.ml_heads * MLSTM_QK_DIM

    @property
    def col_ob(self):
        return self.col_vb + self.ml_width

    @property
    def col_gb(self):
        return self.col_ob + self.ml_width

    @property
    def main_cols(self):
        return self.col_gb + self.ml_width


def _params(*sem):
    return pltpu.CompilerParams(dimension_semantics=sem, vmem_limit_bytes=VMEM_LIMIT)


def _silu(x):
    return x * jax.nn.sigmoid(x)


def _wprep_kernel(w_ref, wg_ref, perm_ref, place_ref, o_ref, og_ref, *, qk_tiles, q_tiles, q_scale):
    j = pl.program_id(1)

    @pl.when(j < qk_tiles)
    def _():
        x = (w_ref[0] * jnp.where(j < q_tiles, q_scale, 1.0)).astype(BF16)
        for r in range(0, x.shape[0], LANES):
            o_ref[0, r:r + LANES, :] = jnp.dot(
                perm_ref[...], x[r:r + LANES, :], preferred_element_type=F32).astype(BF16)

    @pl.when(j >= qk_tiles)
    def _():
        o_ref[0] = w_ref[0].astype(BF16)

    @pl.when(j == 0)
    def _():
        og_ref[0] = jnp.dot(place_ref[...], wg_ref[0].astype(BF16), preferred_element_type=F32).astype(BF16)


def _rope_row_permutation():
    perm = np.zeros((LANES, LANES), np.float32)
    for c in range(2):
        for i in range(ATT_QK_DIM // 2):
            for par in range(2):
                perm[par * ATT_QK_DIM + c * (ATT_QK_DIM // 2) + i, c * ATT_QK_DIM + 2 * i + par] = 1.0
    return jnp.asarray(perm, BF16)


def _gate_row_placement(dims):
    mh = dims.ml_heads
    place = np.zeros((mh * LANES, 4 * mh), np.float32)
    for kind in range(4):
        for h in range(mh):
            place[h * LANES + kind, kind * mh + h] = 1.0
    return jnp.asarray(place, BF16)


def _wprep_call(dims, w_in):
    depth, d, n_all = w_in.shape
    mh = dims.ml_heads
    tn = min(dims.att_width, 1024)
    n_gate = n_all - dims.main_cols
    assert dims.att_width % tn == 0 and dims.main_cols % tn == 0
    assert dims.main_cols % n_gate == 0 and n_gate == 4 * mh and n_gate % 8 == 0
    kernel = functools.partial(_wprep_kernel, qk_tiles=2 * dims.att_width // tn, q_tiles=dims.att_width // tn,
                               q_scale=ATT_QK_DIM ** -0.5 * math.log2(math.e))
    w_t = jnp.swapaxes(w_in, 1, 2)
    return pl.pallas_call(
        kernel,
        grid=(depth, dims.main_cols // tn),
        in_specs=[
            pl.BlockSpec((1, tn, d), lambda l, j: (l, j, 0)),
            pl.BlockSpec((1, n_gate, d), lambda l, j: (l, dims.main_cols // n_gate, 0)),
            pl.BlockSpec((LANES, LANES), lambda l, j: (0, 0)),
            pl.BlockSpec((mh * LANES, n_gate), lambda l, j: (0, 0)),
        ],
        out_specs=[
            pl.BlockSpec((1, tn, d), lambda l, j: (l, j, 0)),
            pl.BlockSpec((1, mh * LANES, d), lambda l, j: (l, 0, 0)),
        ],
        out_shape=[
            jax.ShapeDtypeStruct((depth, dims.main_cols, d), BF16),
            jax.ShapeDtypeStruct((depth, mh * LANES, d), BF16),
        ],
        compiler_params=_params("arbitrary", "arbitrary"),
        name="w_prep",
    )(w_t, w_t, _rope_row_permutation(), _gate_row_placement(dims))


def _cast_kernel(w_ref, o_ref):
    o_ref[...] = w_ref[...].astype(BF16)


def _cast_call(w):
    depth, r, c = w.shape
    tr = min(r, 512)
    return pl.pallas_call(
        _cast_kernel,
        grid=(depth, r // tr),
        in_specs=[pl.BlockSpec((1, tr, c), lambda l, i: (l, i, 0))],
        out_specs=pl.BlockSpec((1, tr, c), lambda l, i: (l, i, 0)),
        out_shape=jax.ShapeDtypeStruct(w.shape, BF16),
        compiler_params=_params("arbitrary", "arbitrary"),
        name="w_cast",
    )(w)


def _ada_kernel(c_ref, w_ref, b_ref, o_ref):
    s = _silu(c_ref[...]).astype(BF16)
    w = w_ref[0].astype(BF16)
    o_ref[0] = jnp.dot(s, w, preferred_element_type=F32) + b_ref[0]


def _ada_call(c_all, w_ada, b_ada):
    depth, d, n = w_ada.shape
    tn = min(n, 1024)
    return pl.pallas_call(
        _ada_kernel,
        grid=(depth, n // tn),
        in_specs=[
            pl.BlockSpec((MOD_ROWS, d), lambda l, j: (0, 0)),
            pl.BlockSpec((1, d, tn), lambda l, j: (l, 0, j)),
            pl.BlockSpec((1, 1, tn), lambda l, j: (l, 0, j)),
        ],
        out_specs=pl.BlockSpec((1, MOD_ROWS, tn), lambda l, j: (l, 0, j)),
        out_shape=jax.ShapeDtypeStruct((depth, MOD_ROWS, n), F32),
        compiler_params=_params("arbitrary", "arbitrary"),
        name="ada",
    )(c_all, w_ada, b_ada.reshape(depth, 1, n))


def _inproj_kernel(x_ref, shift_ref, scale_ref, nw_ref, w_ref, wg_ref, cos_ref, sin_ref, p_ref, g_ref, h_scr,
                   *, tm, tn, rows_per_mod, fixed_mod_row, rope_tiles):
    i = pl.program_id(0)
    j = pl.program_id(1)
    rc = min(tm, 256)

    @pl.when(j == 0)
    def _():
        row = fixed_mod_row if fixed_mod_row is not None else (i * tm) // rows_per_mod
        shift = shift_ref[pl.ds(row, 1), :]
        scale1 = 1.0 + scale_ref[pl.ds(row, 1), :]
        nw = nw_ref[...]
        for r in range(0, tm, rc):
            xf = x_ref[r:r + rc, :]
            ms = jnp.mean(xf * xf, axis=-1, keepdims=True)
            y = xf * lax.rsqrt(ms + EPS) * nw
            h_scr[r:r + rc, :] = (y * scale1 + shift).astype(BF16)
        g_ref[...] = lax.dot_general(h_scr[...], wg_ref[...], NT, preferred_element_type=F32)

    def project(rope):
        for n in range(0, tn, 2 * LANES):
            acc = lax.dot_general(h_scr[...], w_ref[n:n + 2 * LANES, :], NT, preferred_element_type=F32)
            if rope:
                for hh in range(2):
                    a = acc[:, hh * LANES:(hh + 1) * LANES]
                    r = a * cos_ref[...] + pltpu.roll(a, LANES // 2, axis=1) * sin_ref[...]
                    p_ref[:, n + hh * LANES:n + (hh + 1) * LANES] = r.astype(BF16)
            else:
                p_ref[:, n:n + 2 * LANES] = acc.astype(BF16)

    if rope_tiles:
        pl.when(j < rope_tiles)(lambda: project(True))
        pl.when(j >= rope_tiles)(lambda: project(False))
    else:
        project(False)


def _inproj_call(dims, x_flat, mod, nw, w_main, layer, w_gate, cos_t, sin_t, is_ctx):
    m, d = x_flat.shape
    n_main = w_main.shape[1]
    n_gate = w_gate.shape[1]
    tm = min(m if is_ctx else dims.seq, 1024)
    tn = min(n_main, 1024)
    assert (2 * dims.att_width) % tn == 0
    kernel = functools.partial(
        _inproj_kernel, tm=tm, tn=tn, rows_per_mod=dims.seq,
        fixed_mod_row=dims.batch if is_ctx else None,
        rope_tiles=0 if is_ctx else (2 * dims.att_width) // tn)
    tiles_per_seq = max(dims.seq // tm, 1)
    return pl.pallas_call(
        kernel,
        grid=(m // tm, n_main // tn),
        in_specs=[
            pl.BlockSpec((tm, d), lambda i, j: (i, 0)),
            pl.BlockSpec((MOD_ROWS, d), lambda i, j: (0, 0)),
            pl.BlockSpec((MOD_ROWS, d), lambda i, j: (0, 1)),
            pl.BlockSpec((1, d), lambda i, j: (0, 0)),
            pl.BlockSpec((None, tn, d), lambda i, j: (layer, j, 0)),
            pl.BlockSpec((None, n_gate, d), lambda i, j: (layer, 0, 0)),
            pl.BlockSpec((tm, LANES), lambda i, j: (i % tiles_per_seq, 0)),
            pl.BlockSpec((tm, LANES), lambda i, j: (i % tiles_per_seq, 0)),
        ],
        out_specs=[
            pl.BlockSpec((tm, tn), lambda i, j: (i, j)),
            pl.BlockSpec((tm, n_gate), lambda i, j: (i, 0)),
        ],
        out_shape=[
            jax.ShapeDtypeStruct((m, n_main), BF16),
            jax.ShapeDtypeStruct((m, n_gate), F32),
        ],
        scratch_shapes=[pltpu.VMEM((tm, d), BF16)],
        compiler_params=_params("arbitrary", "arbitrary"),
        name="in_proj_ctx" if is_ctx else "in_proj",
    )(x_flat, mod, mod, nw, w_main, w_gate, cos_t, sin_t)


def _lambda(lam_ref, lam_init):
    lp = lam_ref[...]
    return (jnp.exp(jnp.sum(lp[0:1] * lp[1:2], axis=1, keepdims=True))
            - jnp.exp(jnp.sum(lp[2:3] * lp[3:4], axis=1, keepdims=True)) + lam_init)


def _first_sub_head_lanes():
    lane = lax.broadcasted_iota(jnp.int32, (1, LANES), 1)
    return (lane % (LANES // 2)) < (LANES // 4)


def _attn_finish(o_t, lam_init, sub_ref, g):
    ms = jnp.mean(o_t * o_t, axis=0, keepdims=True)
    y = (o_t * lax.rsqrt(ms + EPS)).T
    y = y * sub_ref[...] * (1.0 - lam_init)
    return (y * _silu(g.astype(F32))).astype(BF16)


def _attn_lat_kernel(q_ref, kc_ref, vc_ref, kl_ref, vl_ref, g_ref, lam_ref, sub_ref, o_ref,
                     vt_scr, s_scr, *, lam_init, tq, kc):
    ctx, seq = kc_ref.shape[0], kl_ref.shape[0]
    n_tiles = seq // tq
    n_kc = (ctx + seq) // kc

    for r in range(0, ctx, kc):
        vt_scr[:, r:r + kc] = vc_ref[r:r + kc, :].T
    for r in range(0, seq, kc):
        vt_scr[:, ctx + r:ctx + r + kc] = vl_ref[r:r + kc, :].T

    lam = _lambda(lam_ref, lam_init)
    first = _first_sub_head_lanes()

    def unit_queries(u):
        t, sub = divmod(u, 2)
        q = q_ref[t * tq:(t + 1) * tq, :]
        return jnp.where(first if sub == 0 else jnp.logical_not(first), q, jnp.zeros_like(q))

    def scores(qs, slot):
        s_c = lax.dot_general(kc_ref[...], qs, NT, preferred_element_type=F32)
        s_l = lax.dot_general(kl_ref[...], qs, NT, preferred_element_type=F32)
        s_scr[slot, 0:ctx] = s_c
        s_scr[slot, ctx:] = s_l
        return jnp.maximum(jnp.max(s_c, axis=0, keepdims=True), jnp.max(s_l, axis=0, keepdims=True))

    def softmax_values(slot, m):
        den = jnp.zeros((1, tq), F32)
        o_t = jnp.zeros((ATT_V_DIM, tq), F32)
        for i in range(n_kc):
            p = jnp.exp2(s_scr[slot, i * kc:(i + 1) * kc, :] - m)
            den = den + jnp.sum(p, axis=0, keepdims=True)
            o_t = o_t + jnp.dot(vt_scr[:, i * kc:(i + 1) * kc], p.astype(BF16), preferred_element_type=F32)
        return o_t * (1.0 / den)

    n_units = 2 * n_tiles
    n_slots = s_scr.shape[0]
    ahead = n_slots - 1
    maxima = [scores(unit_queries(u), u % n_slots) for u in range(min(ahead, n_units))]
    o_first = None
    for u in range(n_units):
        if u + ahead < n_units:
            maxima.append(scores(unit_queries(u + ahead), (u + ahead) % n_slots))
        o_t = softmax_values(u % n_slots, maxima[u])
        if u % 2 == 0:
            o_first = o_t
        else:
            rows = slice((u // 2) * tq, (u // 2 + 1) * tq)
            o_ref[rows, :] = _attn_finish(o_first - lam * o_t, lam_init, sub_ref, g_ref[rows, :])


def _attn_ctx_kernel(q_ref, kc_ref, vc_ref, g_ref, lam_ref, sub_ref, o_ref, *, lam_init):
    lam = _lambda(lam_ref, lam_init)
    first = _first_sub_head_lanes()
    for h in range(q_ref.shape[1] // LANES):
        cols = slice(h * LANES, (h + 1) * LANES)
        q = q_ref[:, cols]
        k = kc_ref[:, cols]
        vt = vc_ref[:, cols].T
        outs = []
        for sub in range(2):
            qs = jnp.where(first if sub == 0 else jnp.logical_not(first), q, jnp.zeros_like(q))
            s = lax.dot_general(k, qs, NT, preferred_element_type=F32)
            p = jnp.exp2(s - jnp.max(s, axis=0, keepdims=True))
            den = jnp.sum(p, axis=0, keepdims=True)
            outs.append(jnp.dot(vt, p.astype(BF16), preferred_element_type=F32) * (1.0 / den))
        o_ref[:, cols] = _attn_finish(outs[0] - lam * outs[1], lam_init, sub_ref, g_ref[:, cols])


def _attn_call(dims, p_lat, p_ctx, lam_rows, subln, layer_idx, is_ctx):
    lam_init = 0.8 - 0.6 * math.exp(-0.3 * layer_idx)
    ah = dims.att_heads
    seq, ctx = dims.seq, dims.ctx
    cb = lambda off: off // LANES
    lat = lambda off: pl.BlockSpec((seq, LANES), lambda b, h: (b, cb(off) + h))
    cx = lambda off: pl.BlockSpec((ctx, LANES), lambda b, h: (b, cb(off) + h))
    small = [pl.BlockSpec((4, ATT_QK_DIM), lambda b, h: (0, 0)),
             pl.BlockSpec((1, ATT_V_DIM), lambda b, h: (0, 0))]
    if is_ctx:
        aw = dims.att_width
        wide = lambda off: pl.BlockSpec((ctx, aw), lambda b: (b, off // aw))
        return pl.pallas_call(
            functools.partial(_attn_ctx_kernel, lam_init=lam_init),
            grid=(dims.batch,),
            in_specs=[wide(dims.col_qa), wide(dims.col_ka), wide(dims.col_va), wide(dims.col_ga),
                      pl.BlockSpec((4, ATT_QK_DIM), lambda b: (0, 0)),
                      pl.BlockSpec((1, ATT_V_DIM), lambda b: (0, 0))],
            out_specs=pl.BlockSpec((ctx, aw), lambda b: (b, 0)),
            out_shape=jax.ShapeDtypeStruct((dims.n_ctx, aw), BF16),
            compiler_params=_params("arbitrary"),
            name="attn_ctx",
        )(p_ctx, p_ctx, p_ctx, p_ctx, lam_rows, subln)
    tq = min(seq, 512)
    kc = min(ctx, 256)
    return pl.pallas_call(
        functools.partial(_attn_lat_kernel, lam_init=lam_init, tq=tq, kc=kc),
        grid=(dims.batch, ah),
        in_specs=[lat(dims.col_qa), cx(dims.col_ka), cx(dims.col_va), lat(dims.col_ka), lat(dims.col_va),
                  lat(dims.col_ga)] + small,
        out_specs=pl.BlockSpec((seq, LANES), lambda b, h: (b, h)),
        out_shape=jax.ShapeDtypeStruct((dims.n_lat, dims.att_width), BF16),
        scratch_shapes=[
            pltpu.VMEM((ATT_V_DIM, ctx + seq), BF16),
            pltpu.VMEM((2, ctx + seq, tq), F32),
        ],
        compiler_params=_params("arbitrary", "arbitrary"),
        name="attn",
    )(p_lat, p_ctx, p_ctx, p_lat, p_lat, p_lat, lam_rows, subln)


def _split3(x):
    hi = x.astype(BF16)
    r1 = x - hi.astype(F32)
    mid = r1.astype(BF16)
    lo = (r1 - mid.astype(F32)).astype(BF16)
    return hi, mid, lo


def _conv_silu(src_ref, w, b, scale, dst_ref):
    x = src_ref[...].astype(F32)
    t = x.shape[0]
    row = lax.broadcasted_iota(jnp.int32, (t, 1), 0)
    prev = jnp.where(row == 0, 0.0, pltpu.roll(x, 1, axis=0))
    nxt = jnp.where(row == t - 1, 0.0, pltpu.roll(x, t - 1, axis=0))
    y = b + w[0:1] * prev
    y = y + w[1:2] * x
    y = y + w[2:3] * nxt
    y = _silu(y)
    if scale is not None:
        y = y * scale
    dst_ref[...] = y.astype(BF16)


def _mlstm_kernel(*refs, chunk, with_ctx_out):
    (ql_ref, kl_ref, vl_ref, gl_ref, ol_ref, sl_ref,
     qc_ref, kc_ref, vc_ref, gc_ref, oc_ref, sc_ref,
     cwq_ref, cwk_ref, cbq_ref, cbk_ref, ib_ref, fb_ref, hn_ref) = refs[:19]
    refs = refs[19:]
    if with_ctx_out:
        yl_ref, yc_ref = refs[:2]
        refs = refs[2:]
    else:
        yl_ref, yc_ref = refs[0], None
        refs = refs[1:]
    qs_scr, ks_scr, vt_scr, gr_scr, h_scr, u_scr = refs
    L = chunk
    ctx, seq = qc_ref.shape[0], ql_ref.shape[0]
    ncc, nc = ctx // L, (ctx + seq) // L
    hd = pl.program_id(1)

    k_scale = MLSTM_QK_DIM ** -0.5
    _conv_silu(qc_ref, cwq_ref[...], cbq_ref[...], None, qs_scr.at[0:ctx])
    _conv_silu(kc_ref, cwk_ref[...], cbk_ref[...], k_scale, ks_scr.at[0:ctx])
    _conv_silu(ql_ref, cwq_ref[...], cbq_ref[...], None, qs_scr.at[ctx:])
    _conv_silu(kl_ref, cwk_ref[...], cbk_ref[...], k_scale, ks_scr.at[ctx:])

    def src_rows(c):
        return (True, slice(c * L, (c + 1) * L)) if c < ncc else (False, slice((c - ncc) * L, (c - ncc + 1) * L))

    for c in range(nc):
        is_ctx, rows = src_rows(c)
        vt_scr[c] = (vc_ref if is_ctx else vl_ref)[rows, :].T
        gr_scr[c] = (gc_ref if is_ctx else gl_ref)[rows, :].T[0:8, :]

    row = lax.broadcasted_iota(jnp.int32, (1, 8, 1), 1)
    bias = jnp.where(row == 0, ib_ref[0, hd],
                     jnp.where(row == 1, ib_ref[1, hd],
                               jnp.where(row == 2, fb_ref[0, hd],
                                         jnp.where(row == 3, fb_ref[1, hd], 0.0))))
    capped = GATE_SOFTCAP * jnp.tanh((gr_scr[...] + bias) * (1.0 / GATE_SOFTCAP))
    lsig = jnp.minimum(capped, 0.0) - jnp.log1p(jnp.exp(-jnp.abs(capped)))
    lf = jnp.where((row >= 2) & (row < 4), lsig, 0.0).reshape(nc * 8, L)
    ui = lax.broadcasted_iota(jnp.int32, (L, L), 0)
    ti = lax.broadcasted_iota(jnp.int32, (L, L), 1)
    tri_t = (ui <= ti).astype(BF16)
    hi, mid, lo = _split3(lf)
    prefix = (jnp.dot(hi, tri_t, preferred_element_type=F32) + jnp.dot(mid, tri_t, preferred_element_type=F32)
              + jnp.dot(lo, tri_t, preferred_element_type=F32))
    tot = prefix[:, L - 1:L]
    suffix = tot - prefix + lf
    row2 = lax.broadcasted_iota(jnp.int32, (nc * 8, 1), 0) % 8
    gates = jnp.where(row2 < 2, capped.reshape(nc * 8, L), jnp.where(row2 == 2, prefix, suffix))

    mask_f = ui <= ti
    mask_b = ui >= ti
    zeros_pad = jnp.zeros((LANES - 8, L), F32)

    rows_of = []
    local = [[None] * nc for _ in range(2)]
    for c in range(nc):
        g8 = gates[c * 8:(c + 1) * 8, :]
        t8 = tot[c * 8:(c + 1) * 8, :]
        dcols = jnp.concatenate([g8 - pltpu.roll(g8, 6, axis=0), zeros_pad], axis=0).T
        q = qs_scr[c * L:(c + 1) * L, :]
        k = ks_scr[c * L:(c + 1) * L, :]
        vt = vt_scr[c]
        qk = lax.dot_general(k, q, NT, preferred_element_type=F32)
        per_dir = []
        for d in range(2):
            irow, brow, blast = g8[d:d + 1, :], g8[2 + d:3 + d, :], t8[2 + d:3 + d, :]
            per_dir.append((irow, brow, blast))
            logw = jnp.where(mask_f if d == 0 else mask_b, brow + dcols[:, d:d + 1], -jnp.inf)
            a = jnp.max(logw, axis=0, keepdims=True)
            s = qk * jnp.exp(logw - a)
            h_scr[d, c] = jnp.dot(vt, s.astype(BF16), preferred_element_type=F32)
            dsum = jnp.sum(s, axis=0, keepdims=True)
            grow = blast - brow + irow
            gmax = jnp.max(grow, axis=1, keepdims=True)
            wk = jnp.exp(grow - gmax)
            u_scr[d, c] = jnp.dot(vt * wk.astype(BF16), k, preferred_element_type=F32)
            un = jnp.dot(jnp.broadcast_to(wk, (8, L)).astype(BF16), k, preferred_element_type=F32)
            local[d][c] = (a, dsum, gmax, un)
        rows_of.append(per_dir)

    order = [list(range(nc)), list(range(ncc - 1, -1, -1)) + list(range(nc - 1, ncc - 1, -1))]
    for d in range(2):
        ct = jnp.zeros((MLSTM_V_DIM, MLSTM_QK_DIM), F32)
        n = jnp.zeros((8, MLSTM_QK_DIM), F32)
        m = jnp.zeros((1, 1), F32)
        for c in order[d]:
            irow, brow, blast = rows_of[c][d]
            a, dsum, gmax, un = local[d][c]
            q = qs_scr[c * L:(c + 1) * L, :]
            inter = brow + m
            m_t = jnp.maximum(inter, a)
            e_inter = jnp.exp(inter - m_t)
            e_intra = jnp.exp(a - m_t)
            qc = lax.dot_general(ct.astype(BF16), q, NT, preferred_element_type=F32)
            qn = lax.dot_general(n.astype(BF16), q, NT, preferred_element_type=F32)[0:1, :]
            numer = e_inter * qc + e_intra * h_scr[d, c]
            denom = e_inter * qn + e_intra * dsum
            h_scr[d, c] = numer * (1.0 / jnp.maximum(jnp.abs(denom), jnp.exp(-m_t)))
            m_new = jnp.maximum(blast + m, gmax)
            decay = jnp.exp(blast + m - m_new)
            gain = jnp.exp(gmax - m_new)
            ct = decay * ct + gain * u_scr[d, c]
            n = decay * n + gain * un
            m = m_new

    hn = hn_ref[...]
    for c in range(nc):
        is_ctx, rows = src_rows(c)
        if is_ctx and not with_ctx_out:
            continue
        o_ref, s_ref, y_ref = (oc_ref, sc_ref, yc_ref) if is_ctx else (ol_ref, sl_ref, yl_ref)
        hsum = h_scr[0, c] + h_scr[1, c]
        ms = jnp.mean(hsum * hsum, axis=0, keepdims=True)
        y = (hsum * lax.rsqrt(ms + EPS)).T * hn
        y = y * jax.nn.sigmoid(o_ref[rows, :].astype(F32))
        y_ref[rows, :] = (y * _silu(s_ref[rows, :].astype(F32))).astype(BF16)


def _mlstm_call(dims, p_lat, g_lat, p_ctx, g_ctx, conv_w, conv_b, i_bias, f_bias, head_norm, with_ctx_out):
    mh = dims.ml_heads
    seq, ctx = dims.seq, dims.ctx
    L = min(ctx, 256)
    qk, dv = MLSTM_QK_DIM, MLSTM_V_DIM
    lat = lambda width, off: pl.BlockSpec((seq, width), lambda b, h: (b, off // width + h))
    cx = lambda width, off: pl.BlockSpec((ctx, width), lambda b, h: (b, off // width + h))
    smem = pl.BlockSpec(memory_space=pltpu.SMEM)
    in_specs = [
        lat(qk, dims.col_qb), lat(qk, dims.col_kb), lat(dv, dims.col_vb), lat(LANES, 0),
        lat(dv, dims.col_ob), lat(dv, dims.col_gb),
        cx(qk, dims.col_qb), cx(qk, dims.col_kb), cx(dv, dims.col_vb), cx(LANES, 0),
        cx(dv, dims.col_ob), cx(dv, dims.col_gb),
        pl.BlockSpec((CONV_WIDTH, qk), lambda b, h: (0, h)),
        pl.BlockSpec((CONV_WIDTH, qk), lambda b, h: (0, mh + h)),
        pl.BlockSpec((1, qk), lambda b, h: (0, h)),
        pl.BlockSpec((1, qk), lambda b, h: (0, mh + h)),
        smem, smem,
        pl.BlockSpec((1, dv), lambda b, h: (0, h)),
    ]
    args = [p_lat, p_lat, p_lat, g_lat, p_lat, p_lat, p_ctx, p_ctx, p_ctx, g_ctx, p_ctx, p_ctx,
            conv_w, conv_w, conv_b, conv_b, i_bias, f_bias, head_norm]
    out_specs = [pl.BlockSpec((seq, dv), lambda b, h: (b, h))]
    out_shape = [jax.ShapeDtypeStruct((dims.n_lat, dims.ml_width), BF16)]
    if with_ctx_out:
        out_specs.append(pl.BlockSpec((ctx, dv), lambda b, h: (b, h)))
        out_shape.append(jax.ShapeDtypeStruct((dims.n_ctx, dims.ml_width), BF16))
    nc = (ctx + seq) // L
    scratch = [
        pltpu.VMEM((ctx + seq, qk), BF16), pltpu.VMEM((ctx + seq, qk), BF16),
        pltpu.VMEM((nc, dv, L), BF16),
        pltpu.VMEM((nc, 8, L), F32),
        pltpu.VMEM((2, nc, dv, L), F32),
        pltpu.VMEM((2, nc, dv, qk), F32),
    ]
    return pl.pallas_call(
        functools.partial(_mlstm_kernel, chunk=L, with_ctx_out=with_ctx_out),
        grid=(dims.batch, mh),
        in_specs=in_specs,
        out_specs=out_specs,
        out_shape=out_shape,
        scratch_shapes=scratch,
        compiler_params=_params("arbitrary", "arbitrary"),
        name="mlstm",
    )(*args)


def _outproj_kernel(ya_ref, ym_ref, w_ref, x_ref, gate_ref, nw_ref, o_ref, *, tm, rows_per_mod, fixed_mod_row):
    i = pl.program_id(0)
    d = o_ref.shape[1]
    ka = ya_ref.shape[1]
    nc = min(d, 512)
    ss = jnp.zeros((tm, 1), F32)
    for n in range(0, d, nc):
        acc = (jnp.dot(ya_ref[...], w_ref[0:ka, n:n + nc], preferred_element_type=F32)
               + jnp.dot(ym_ref[...], w_ref[ka:, n:n + nc], preferred_element_type=F32))
        ss = ss + jnp.sum(acc * acc, axis=1, keepdims=True)
        o_ref[:, n:n + nc] = acc
    row = fixed_mod_row if fixed_mod_row is not None else (i * tm) // rows_per_mod
    gate = gate_ref[pl.ds(row, 1), :]
    inv = lax.rsqrt(ss * (1.0 / d) + EPS)
    o_ref[...] = x_ref[...] + gate * (o_ref[...] * inv * nw_ref[...])


def _outproj_call(dims, y_att, y_ml, w_out, layer, x_flat, mod, nw, is_ctx):
    m, d = x_flat.shape
    tm = min(m, 512)
    kernel = functools.partial(_outproj_kernel, tm=tm, rows_per_mod=dims.seq,
                               fixed_mod_row=dims.batch if is_ctx else None)
    return pl.pallas_call(
        kernel,
        grid=(m // tm,),
        in_specs=[
            pl.BlockSpec((tm, y_att.shape[1]), lambda i: (i, 0)),
            pl.BlockSpec((tm, y_ml.shape[1]), lambda i: (i, 0)),
            pl.BlockSpec((None, d, d), lambda i: (layer, 0, 0)),
            pl.BlockSpec((tm, d), lambda i: (i, 0)),
            pl.BlockSpec((MOD_ROWS, d), lambda i: (0, 2)),
            pl.BlockSpec((1, d), lambda i: (0, 0)),
        ],
        out_specs=pl.BlockSpec((tm, d), lambda i: (i, 0)),
        out_shape=jax.ShapeDtypeStruct((m, d), F32),
        compiler_params=_params("arbitrary"),
        name="out_proj_ctx" if is_ctx else "out_proj",
    )(y_att, y_ml, w_out, x_flat, mod, nw)


def _rope_tables(seq):
    rows = seq // GRID_W
    row = jnp.repeat(jnp.arange(rows), GRID_W).astype(F32)
    col = jnp.tile(jnp.arange(GRID_W), rows).astype(F32)
    n_freq = ATT_QK_DIM // 4
    inv = ROPE_BASE ** (-jnp.arange(n_freq, dtype=F32) / n_freq)
    ang = jnp.concatenate([row[:, None] * inv, col[:, None] * inv], axis=-1)
    cos, sin = jnp.cos(ang), jnp.sin(ang)
    return jnp.tile(cos, (1, 4)), jnp.concatenate([-sin, -sin, sin, sin], axis=-1)


def _forward(dims, x, c, ctx, c_ctx, w_ada, b_ada, norm_pre, norm_post, w_in, w_out,
             lam_q1, lam_k1, lam_q2, lam_k2, attn_subln, conv_w, conv_b, i_bias, f_bias, mlstm_norm):
    depth = w_ada.shape[0]
    d = dims.d_model
    x_flat = x.reshape(dims.n_lat, d)
    xc_flat = ctx.reshape(dims.n_ctx, d)
    c_all = jnp.concatenate(
        [c, c_ctx[None], jnp.zeros((MOD_ROWS - dims.batch - 1, d), F32)], axis=0)
    mod_all = _ada_call(c_all, w_ada, b_ada)
    w_main, w_gate = _wprep_call(dims, w_in)
    w_out_b = _cast_call(w_out)
    cos_t, sin_t = _rope_tables(dims.seq)
    for l in range(depth):
        need_ctx = l < depth - 1
        mod = mod_all[l]
        nw_pre = norm_pre[l].reshape(1, d)
        p_lat, g_lat = _inproj_call(dims, x_flat, mod, nw_pre, w_main, l, w_gate, cos_t, sin_t, False)
        p_ctx, g_ctx = _inproj_call(dims, xc_flat, mod, nw_pre, w_main, l, w_gate, cos_t, sin_t, True)
        lam_rows = jnp.stack([lam_q1[l], lam_k1[l], lam_q2[l], lam_k2[l]])
        subln = attn_subln[l].reshape(1, ATT_V_DIM)
        ya_lat = _attn_call(dims, p_lat, p_ctx, lam_rows, subln, l, False)
        ym = _mlstm_call(dims, p_lat, g_lat, p_ctx, g_ctx, conv_w[l], conv_b[l].reshape(1, -1),
                         i_bias[l], f_bias[l], mlstm_norm[l].reshape(1, -1), need_ctx)
        nw_post = norm_post[l].reshape(1, d)
        x_new = _outproj_call(dims, ya_lat, ym[0], w_out_b, l, x_flat, mod, nw_post, False)
        if need_ctx:
            ya_ctx = _attn_call(dims, p_lat, p_ctx, lam_rows, subln, l, True)
            xc_flat = _outproj_call(dims, ya_ctx, ym[1], w_out_b, l, xc_flat, mod, nw_post, True)
        x_flat = x_new
    return x_flat.reshape(x.shape)


def kernel(x, c, ctx, c_ctx, w_ada, b_ada, norm_pre, norm_post, w_in, w_out, lam_q1, lam_k1, lam_q2, lam_k2,
           attn_subln, conv_w, conv_b, i_bias, f_bias, mlstm_norm):
    dims = Dims(batch=x.shape[0], seq=x.shape[1], ctx=ctx.shape[1], d_model=x.shape[2])
    return _forward(dims, x, c, ctx, c_ctx, w_ada, b_ada, norm_pre, norm_post, w_in, w_out,
                    lam_q1, lam_k1, lam_q2, lam_k2, attn_subln, conv_w, conv_b, i_bias, f_bias, mlstm_norm)
```

```python
import functools
import math
from typing import NamedTuple

import numpy as np
import jax
import jax.numpy as jnp
from jax import lax
from jax.experimental import pallas as pl
from jax.experimental.pallas import tpu as pltpu

F32 = jnp.float32
BF16 = jnp.bfloat16

LANES = 128
ATT_V_DIM = 128
ATT_QK_DIM = 64
MLSTM_V_DIM = 256
MLSTM_QK_DIM = 128
CONV_WIDTH = 3
GATE_SOFTCAP = 15.0
ROPE_BASE = 10000.0
GRID_W = 64
EPS = 1e-6
MOD_ROWS = 16
VMEM_LIMIT = 56 * 1024 * 1024
NT = (((1,), (1,)), ((), ()))
MIN_SOFTMAX_DENOMINATOR = 2.0 ** -60


class Dims(NamedTuple):
    batch: int
    seq: int
    ctx: int
    d_model: int

    @property
    def att_width(self):
        return self.d_model // 2

    @property
    def att_heads(self):
        return self.att_width // ATT_V_DIM

    @property
    def ml_width(self):
        return self.d_model - self.att_width

    @property
    def ml_heads(self):
        return self.ml_width // MLSTM_V_DIM

    @property
    def n_lat(self):
        return self.batch * self.seq

    @property
    def n_ctx(self):
        return self.batch * self.ctx

    @property
    def col_qa(self):
        return 0

    @property
    def col_ka(self):
        return self.att_width

    @property
    def col_va(self):
        return 2 * self.att_width

    @property
    def col_ga(self):
        return 3 * self.att_width

    @property
    def col_qb(self):
        return 4 * self.att_width

    @property
    def col_kb(self):
        return self.col_qb + self.ml_heads * MLSTM_QK_DIM

    @property
    def col_vb(self):
        return self.col_kb + self.ml_heads * MLSTM_QK_DIM

    @property
    def col_ob(self):
        return self.col_vb + self.ml_width

    @property
    def col_gb(self):
        return self.col_ob + self.ml_width

    @property
    def main_cols(self):
        return self.col_gb + self.ml_width


def _params(*sem):
    return pltpu.CompilerParams(dimension_semantics=sem, vmem_limit_bytes=VMEM_LIMIT)


def _silu(x):
    return x * jax.nn.sigmoid(x)


def _wprep_kernel(w_ref, wg_ref, perm_ref, place_ref, o_ref, og_ref, *, qk_tiles, q_tiles, q_scale):
    j = pl.program_id(1)

    @pl.when(j < qk_tiles)
    def _():
        x = (w_ref[0] * jnp.where(j < q_tiles, q_scale, 1.0)).astype(BF16)
        for r in range(0, x.shape[0], LANES):
            o_ref[0, r:r + LANES, :] = jnp.dot(
                perm_ref[...], x[r:r + LANES, :], preferred_element_type=F32).astype(BF16)

    @pl.when(j >= qk_tiles)
    def _():
        o_ref[0] = w_ref[0].astype(BF16)

    @pl.when(j == 0)
    def _():
        og_ref[0] = jnp.dot(place_ref[...], wg_ref[0].astype(BF16), preferred_element_type=F32).astype(BF16)


def _rope_row_permutation():
    perm = np.zeros((LANES, LANES), np.float32)
    for c in range(2):
        for i in range(ATT_QK_DIM // 2):
            for par in range(2):
                perm[par * ATT_QK_DIM + c * (ATT_QK_DIM // 2) + i, c * ATT_QK_DIM + 2 * i + par] = 1.0
    return jnp.asarray(perm, BF16)


def _gate_row_placement(dims):
    mh = dims.ml_heads
    place = np.zeros((mh * LANES, 4 * mh), np.float32)
    for kind in range(4):
        for h in range(mh):
            place[h * LANES + kind, kind * mh + h] = 1.0
    return jnp.asarray(place, BF16)


def _wprep_call(dims, w_in):
    depth, d, n_all = w_in.shape
    mh = dims.ml_heads
    tn = min(dims.att_width, 1024)
    n_gate = n_all - dims.main_cols
    assert dims.att_width % tn == 0 and dims.main_cols % tn == 0
    assert dims.main_cols % n_gate == 0 and n_gate == 4 * mh and n_gate % 8 == 0
    kernel = functools.partial(_wprep_kernel, qk_tiles=2 * dims.att_width // tn, q_tiles=dims.att_width // tn,
                               q_scale=ATT_QK_DIM ** -0.5 * math.log2(math.e))
    w_t = jnp.swapaxes(w_in, 1, 2)
    return pl.pallas_call(
        kernel,
        grid=(depth, dims.main_cols // tn),
        in_specs=[
            pl.BlockSpec((1, tn, d), lambda l, j: (l, j, 0)),
            pl.BlockSpec((1, n_gate, d), lambda l, j: (l, dims.main_cols // n_gate, 0)),
            pl.BlockSpec((LANES, LANES), lambda l, j: (0, 0)),
            pl.BlockSpec((mh * LANES, n_gate), lambda l, j: (0, 0)),
        ],
        out_specs=[
            pl.BlockSpec((1, tn, d), lambda l, j: (l, j, 0)),
            pl.BlockSpec((1, mh * LANES, d), lambda l, j: (l, 0, 0)),
        ],
        out_shape=[
            jax.ShapeDtypeStruct((depth, dims.main_cols, d), BF16),
            jax.ShapeDtypeStruct((depth, mh * LANES, d), BF16),
        ],
        compiler_params=_params("arbitrary", "arbitrary"),
        name="w_prep",
    )(w_t, w_t, _rope_row_permutation(), _gate_row_placement(dims))


def _cast_kernel(w_ref, o_ref):
    o_ref[...] = w_ref[...].astype(BF16)


def _cast_call(w):
    depth, r, c = w.shape
    tr = min(r, 512)
    return pl.pallas_call(
        _cast_kernel,
        grid=(depth, r // tr),
        in_specs=[pl.BlockSpec((1, tr, c), lambda l, i: (l, i, 0))],
        out_specs=pl.BlockSpec((1, tr, c), lambda l, i: (l, i, 0)),
        out_shape=jax.ShapeDtypeStruct(w.shape, BF16),
        compiler_params=_params("arbitrary", "arbitrary"),
        name="w_cast",
    )(w)


def _ada_kernel(c_ref, w_ref, b_ref, o_ref):
    s = _silu(c_ref[...]).astype(BF16)
    w = w_ref[0].astype(BF16)
    o_ref[0] = jnp.dot(s, w, preferred_element_type=F32) + b_ref[0]


def _ada_call(c_all, w_ada, b_ada):
    depth, d, n = w_ada.shape
    tn = min(n, 1024)
    return pl.pallas_call(
        _ada_kernel,
        grid=(depth, n // tn),
        in_specs=[
            pl.BlockSpec((MOD_ROWS, d), lambda l, j: (0, 0)),
            pl.BlockSpec((1, d, tn), lambda l, j: (l, 0, j)),
            pl.BlockSpec((1, 1, tn), lambda l, j: (l, 0, j)),
        ],
        out_specs=pl.BlockSpec((1, MOD_ROWS, tn), lambda l, j: (l, 0, j)),
        out_shape=jax.ShapeDtypeStruct((depth, MOD_ROWS, n), F32),
        compiler_params=_params("arbitrary", "arbitrary"),
        name="ada",
    )(c_all, w_ada, b_ada.reshape(depth, 1, n))


def _inproj_kernel(x_ref, shift_ref, scale_ref, nw_ref, w_ref, wg_ref, cos_ref, sin_ref, p_ref, g_ref, h_scr,
                   *, tm, tn, rows_per_mod, fixed_mod_row, rope_tiles):
    i = pl.program_id(0)
    j = pl.program_id(1)
    rc = min(tm, 256)

    @pl.when(j == 0)
    def _():
        row = fixed_mod_row if fixed_mod_row is not None else (i * tm) // rows_per_mod
        shift = shift_ref[pl.ds(row, 1), :]
        scale1 = 1.0 + scale_ref[pl.ds(row, 1), :]
        nw = nw_ref[...]
        for r in range(0, tm, rc):
            xf = x_ref[r:r + rc, :]
            ms = jnp.mean(xf * xf, axis=-1, keepdims=True)
            y = xf * lax.rsqrt(ms + EPS) * nw
            h_scr[r:r + rc, :] = (y * scale1 + shift).astype(BF16)
        g_ref[...] = lax.dot_general(h_scr[...], wg_ref[...], NT, preferred_element_type=F32)

    def project(rope):
        for n in range(0, tn, 2 * LANES):
            acc = lax.dot_general(h_scr[...], w_ref[n:n + 2 * LANES, :], NT, preferred_element_type=F32)
            if rope:
                for hh in range(2):
                    a = acc[:, hh * LANES:(hh + 1) * LANES]
                    r = a * cos_ref[...] + pltpu.roll(a, LANES // 2, axis=1) * sin_ref[...]
                    p_ref[:, n + hh * LANES:n + (hh + 1) * LANES] = r.astype(BF16)
            else:
                p_ref[:, n:n + 2 * LANES] = acc.astype(BF16)

    if rope_tiles:
        pl.when(j < rope_tiles)(lambda: project(True))
        pl.when(j >= rope_tiles)(lambda: project(False))
    else:
        project(False)


def _inproj_call(dims, x_flat, mod, nw, w_main, layer, w_gate, cos_t, sin_t, is_ctx):
    m, d = x_flat.shape
    n_main = w_main.shape[1]
    n_gate = w_gate.shape[1]
    tm = min(m if is_ctx else dims.seq, 1024)
    tn = min(n_main, 1024)
    assert (2 * dims.att_width) % tn == 0
    kernel = functools.partial(
        _inproj_kernel, tm=tm, tn=tn, rows_per_mod=dims.seq,
        fixed_mod_row=dims.batch if is_ctx else None,
        rope_tiles=0 if is_ctx else (2 * dims.att_width) // tn)
    tiles_per_seq = max(dims.seq // tm, 1)
    return pl.pallas_call(
        kernel,
        grid=(m // tm, n_main // tn),
        in_specs=[
            pl.BlockSpec((tm, d), lambda i, j: (i, 0)),
            pl.BlockSpec((MOD_ROWS, d), lambda i, j: (0, 0)),
            pl.BlockSpec((MOD_ROWS, d), lambda i, j: (0, 1)),
            pl.BlockSpec((1, d), lambda i, j: (0, 0)),
            pl.BlockSpec((None, tn, d), lambda i, j: (layer, j, 0)),
            pl.BlockSpec((None, n_gate, d), lambda i, j: (layer, 0, 0)),
            pl.BlockSpec((tm, LANES), lambda i, j: (i % tiles_per_seq, 0)),
            pl.BlockSpec((tm, LANES), lambda i, j: (i % tiles_per_seq, 0)),
        ],
        out_specs=[
            pl.BlockSpec((tm, tn), lambda i, j: (i, j)),
            pl.BlockSpec((tm, n_gate), lambda i, j: (i, 0)),
        ],
        out_shape=[
            jax.ShapeDtypeStruct((m, n_main), BF16),
            jax.ShapeDtypeStruct((m, n_gate), F32),
        ],
        scratch_shapes=[pltpu.VMEM((tm, d), BF16)],
        compiler_params=_params("arbitrary", "arbitrary"),
        name="in_proj_ctx" if is_ctx else "in_proj",
    )(x_flat, mod, mod, nw, w_main, w_gate, cos_t, sin_t)


def _lambda(lam_ref, lam_init):
    lp = lam_ref[...]
    return (jnp.exp(jnp.sum(lp[0:1] * lp[1:2], axis=1, keepdims=True))
            - jnp.exp(jnp.sum(lp[2:3] * lp[3:4], axis=1, keepdims=True)) + lam_init)


def _first_sub_head_lanes():
    lane = lax.broadcasted_iota(jnp.int32, (1, LANES), 1)
    return (lane % (LANES // 2)) < (LANES // 4)


def _attn_finish(o_t, lam_init, sub_ref, g):
    ms = jnp.mean(o_t * o_t, axis=0, keepdims=True)
    y = (o_t * lax.rsqrt(ms + EPS)).T
    y = y * sub_ref[...] * (1.0 - lam_init)
    return (y * _silu(g.astype(F32))).astype(BF16)


def _attn_lat_kernel(q_ref, kc_ref, vc_ref, kl_ref, vl_ref, g_ref, lam_ref, sub_ref, o_ref,
                     vt_scr, s_scr, redo_scr, *, lam_init, tq, kc):
    ctx, seq = kc_ref.shape[0], kl_ref.shape[0]
    n_tiles = seq // tq
    n_kc = (ctx + seq) // kc

    for r in range(0, ctx, kc):
        vt_scr[:, r:r + kc] = vc_ref[r:r + kc, :].T
    for r in range(0, seq, kc):
        vt_scr[:, ctx + r:ctx + r + kc] = vl_ref[r:r + kc, :].T

    lam = _lambda(lam_ref, lam_init)
    first = _first_sub_head_lanes()

    row8 = lax.broadcasted_iota(jnp.int32, (8, LANES), 0)
    pick = jnp.where(jnp.logical_or(jnp.logical_and(row8 == 0, first),
                                    jnp.logical_and(row8 == 1, jnp.logical_not(first))), 1.0, 0.0).astype(BF16)

    def sq_norms(x):
        return lax.dot_general(pick, x * x, NT, preferred_element_type=F32)

    k_max2 = jnp.maximum(jnp.max(sq_norms(kc_ref[...]), axis=1, keepdims=True),
                         jnp.max(sq_norms(kl_ref[...]), axis=1, keepdims=True))

    def unit_queries(u):
        t, sub = divmod(u, 2)
        q = q_ref[t * tq:(t + 1) * tq, :]
        bound = jnp.sqrt(sq_norms(q)[sub:sub + 1, :] * k_max2[sub:sub + 1, :]) * 1.05
        return jnp.where(first if sub == 0 else jnp.logical_not(first), q, jnp.zeros_like(q)), bound

    def scores(qs, slot):
        s_c = lax.dot_general(kc_ref[...], qs, NT, preferred_element_type=F32)
        s_l = lax.dot_general(kl_ref[...], qs, NT, preferred_element_type=F32)
        s_scr[slot, 0:ctx] = s_c
        s_scr[slot, ctx:] = s_l
        return s_c, s_l

    def softmax_values(slot, m):
        den = jnp.zeros((1, tq), F32)
        o_t = jnp.zeros((ATT_V_DIM, tq), F32)
        for i in range(n_kc):
            p = jnp.exp2(s_scr[slot, i * kc:(i + 1) * kc, :] - m)
            den = den + jnp.sum(p, axis=0, keepdims=True)
            o_t = o_t + jnp.dot(vt_scr[:, i * kc:(i + 1) * kc], p.astype(BF16), preferred_element_type=F32)
        return o_t * (1.0 / den), den

    n_units = 2 * n_tiles
    o_first = None
    den_min = None
    for u in range(n_units):
        qs, bound = unit_queries(u)
        scores(qs, u % 2)
        o_t, den = softmax_values(u % 2, bound)
        den_min = den if den_min is None else jnp.minimum(den_min, den)
        if u % 2 == 0:
            o_first = o_t
        else:
            rows = slice((u // 2) * tq, (u // 2 + 1) * tq)
            o_ref[rows, :] = _attn_finish(o_first - lam * o_t, lam_init, sub_ref, g_ref[rows, :])

    ok = jnp.min(den_min, axis=1, keepdims=True) >= MIN_SOFTMAX_DENOMINATOR
    redo_scr[0] = jnp.where(ok, 0, 1)[0, 0]

    @pl.when(redo_scr[0] != 0)
    def _():
        def exact_tile(t, carry):
            rows = pl.ds(pl.multiple_of(t * tq, tq), tq)
            q = q_ref[rows, :]
            outs = []
            for sub in range(2):
                qs = jnp.where(first if sub == 0 else jnp.logical_not(first), q, jnp.zeros_like(q))
                s_c, s_l = scores(qs, sub)
                m = jnp.maximum(jnp.max(s_c, axis=0, keepdims=True), jnp.max(s_l, axis=0, keepdims=True))
                outs.append(softmax_values(sub, m)[0])
            o_ref[rows, :] = _attn_finish(outs[0] - lam * outs[1], lam_init, sub_ref, g_ref[rows, :])
            return carry

        lax.fori_loop(0, n_tiles, exact_tile, 0)


def _attn_ctx_kernel(q_ref, kc_ref, vc_ref, g_ref, lam_ref, sub_ref, o_ref, *, lam_init):
    lam = _lambda(lam_ref, lam_init)
    first = _first_sub_head_lanes()
    for h in range(q_ref.shape[1] // LANES):
        cols = slice(h * LANES, (h + 1) * LANES)
        q = q_ref[:, cols]
        k = kc_ref[:, cols]
        vt = vc_ref[:, cols].T
        outs = []
        for sub in range(2):
            qs = jnp.where(first if sub == 0 else jnp.logical_not(first), q, jnp.zeros_like(q))
            s = lax.dot_general(k, qs, NT, preferred_element_type=F32)
            p = jnp.exp2(s - jnp.max(s, axis=0, keepdims=True))
            den = jnp.sum(p, axis=0, keepdims=True)
            outs.append(jnp.dot(vt, p.astype(BF16), preferred_element_type=F32) * (1.0 / den))
        o_ref[:, cols] = _attn_finish(outs[0] - lam * outs[1], lam_init, sub_ref, g_ref[:, cols])


def _attn_call(dims, p_lat, p_ctx, lam_rows, subln, layer_idx, is_ctx):
    lam_init = 0.8 - 0.6 * math.exp(-0.3 * layer_idx)
    ah = dims.att_heads
    seq, ctx = dims.seq, dims.ctx
    cb = lambda off: off // LANES
    lat = lambda off: pl.BlockSpec((seq, LANES), lambda b, h: (b, cb(off) + h))
    cx = lambda off: pl.BlockSpec((ctx, LANES), lambda b, h: (b, cb(off) + h))
    small = [pl.BlockSpec((4, ATT_QK_DIM), lambda b, h: (0, 0)),
             pl.BlockSpec((1, ATT_V_DIM), lambda b, h: (0, 0))]
    if is_ctx:
        aw = dims.att_width
        wide = lambda off: pl.BlockSpec((ctx, aw), lambda b: (b, off // aw))
        return pl.pallas_call(
            functools.partial(_attn_ctx_kernel, lam_init=lam_init),
            grid=(dims.batch,),
            in_specs=[wide(dims.col_qa), wide(dims.col_ka), wide(dims.col_va), wide(dims.col_ga),
                      pl.BlockSpec((4, ATT_QK_DIM), lambda b: (0, 0)),
                      pl.BlockSpec((1, ATT_V_DIM), lambda b: (0, 0))],
            out_specs=pl.BlockSpec((ctx, aw), lambda b: (b, 0)),
            out_shape=jax.ShapeDtypeStruct((dims.n_ctx, aw), BF16),
            compiler_params=_params("arbitrary"),
            name="attn_ctx",
        )(p_ctx, p_ctx, p_ctx, p_ctx, lam_rows, subln)
    tq = min(seq, 512)
    kc = min(ctx, 256)
    return pl.pallas_call(
        functools.partial(_attn_lat_kernel, lam_init=lam_init, tq=tq, kc=kc),
        grid=(dims.batch, ah),
        in_specs=[lat(dims.col_qa), cx(dims.col_ka), cx(dims.col_va), lat(dims.col_ka), lat(dims.col_va),
                  lat(dims.col_ga)] + small,
        out_specs=pl.BlockSpec((seq, LANES), lambda b, h: (b, h)),
        out_shape=jax.ShapeDtypeStruct((dims.n_lat, dims.att_width), BF16),
        scratch_shapes=[
            pltpu.VMEM((ATT_V_DIM, ctx + seq), BF16),
            pltpu.VMEM((2, ctx + seq, tq), F32),
            pltpu.SMEM((1,), jnp.int32),
        ],
        compiler_params=_params("arbitrary", "arbitrary"),
        name="attn",
    )(p_lat, p_ctx, p_ctx, p_lat, p_lat, p_lat, lam_rows, subln)


def _split3(x):
    hi = x.astype(BF16)
    r1 = x - hi.astype(F32)
    mid = r1.astype(BF16)
    lo = (r1 - mid.astype(F32)).astype(BF16)
    return hi, mid, lo


def _conv_silu(src_ref, w, b, scale, dst_ref):
    x = src_ref[...].astype(F32)
    t = x.shape[0]
    row = lax.broadcasted_iota(jnp.int32, (t, 1), 0)
    prev = jnp.where(row == 0, 0.0, pltpu.roll(x, 1, axis=0))
    nxt = jnp.where(row == t - 1, 0.0, pltpu.roll(x, t - 1, axis=0))
    y = b + w[0:1] * prev
    y = y + w[1:2] * x
    y = y + w[2:3] * nxt
    y = _silu(y)
    if scale is not None:
        y = y * scale
    dst_ref[...] = y.astype(BF16)


def _mlstm_kernel(*refs, chunk, with_ctx_out):
    (ql_ref, kl_ref, vl_ref, gl_ref, ol_ref, sl_ref,
     qc_ref, kc_ref, vc_ref, gc_ref, oc_ref, sc_ref,
     cwq_ref, cwk_ref, cbq_ref, cbk_ref, ib_ref, fb_ref, hn_ref) = refs[:19]
    refs = refs[19:]
    if with_ctx_out:
        yl_ref, yc_ref = refs[:2]
        refs = refs[2:]
    else:
        yl_ref, yc_ref = refs[0], None
        refs = refs[1:]
    qs_scr, ks_scr, vt_scr, gr_scr, h_scr, u_scr = refs
    L = chunk
    ctx, seq = qc_ref.shape[0], ql_ref.shape[0]
    ncc, nc = ctx // L, (ctx + seq) // L
    hd = pl.program_id(1)

    k_scale = MLSTM_QK_DIM ** -0.5
    _conv_silu(qc_ref, cwq_ref[...], cbq_ref[...], None, qs_scr.at[0:ctx])
    _conv_silu(kc_ref, cwk_ref[...], cbk_ref[...], k_scale, ks_scr.at[0:ctx])
    _conv_silu(ql_ref, cwq_ref[...], cbq_ref[...], None, qs_scr.at[ctx:])
    _conv_silu(kl_ref, cwk_ref[...], cbk_ref[...], k_scale, ks_scr.at[ctx:])

    def src_rows(c):
        return (True, slice(c * L, (c + 1) * L)) if c < ncc else (False, slice((c - ncc) * L, (c - ncc + 1) * L))

    for c in range(nc):
        is_ctx, rows = src_rows(c)
        vt_scr[c] = (vc_ref if is_ctx else vl_ref)[rows, :].T
        gr_scr[c] = (gc_ref if is_ctx else gl_ref)[rows, :].T[0:8, :]

    row = lax.broadcasted_iota(jnp.int32, (1, 8, 1), 1)
    bias = jnp.where(row == 0, ib_ref[0, hd],
                     jnp.where(row == 1, ib_ref[1, hd],
                               jnp.where(row == 2, fb_ref[0, hd],
                                         jnp.where(row == 3, fb_ref[1, hd], 0.0))))
    capped = GATE_SOFTCAP * jnp.tanh((gr_scr[...] + bias) * (1.0 / GATE_SOFTCAP))
    lsig = jnp.minimum(capped, 0.0) - jnp.log1p(jnp.exp(-jnp.abs(capped)))
    lf = jnp.where((row >= 2) & (row < 4), lsig, 0.0).reshape(nc * 8, L)
    ui = lax.broadcasted_iota(jnp.int32, (L, L), 0)
    ti = lax.broadcasted_iota(jnp.int32, (L, L), 1)
    tri_t = (ui <= ti).astype(BF16)
    hi, mid, lo = _split3(lf)
    prefix = (jnp.dot(hi, tri_t, preferred_element_type=F32) + jnp.dot(mid, tri_t, preferred_element_type=F32)
              + jnp.dot(lo, tri_t, preferred_element_type=F32))
    tot = prefix[:, L - 1:L]
    suffix = tot - prefix + lf
    row2 = lax.broadcasted_iota(jnp.int32, (nc * 8, 1), 0) % 8
    gates = jnp.where(row2 < 2, capped.reshape(nc * 8, L), jnp.where(row2 == 2, prefix, suffix))

    mask_f = ui <= ti
    mask_b = ui >= ti
    zeros_pad = jnp.zeros((LANES - 8, L), F32)

    rows_of = []
    local = [[None] * nc for _ in range(2)]
    for c in range(nc):
        g8 = gates[c * 8:(c + 1) * 8, :]
        t8 = tot[c * 8:(c + 1) * 8, :]
        dcols = jnp.concatenate([g8 - pltpu.roll(g8, 6, axis=0), zeros_pad], axis=0).T
        q = qs_scr[c * L:(c + 1) * L, :]
        k = ks_scr[c * L:(c + 1) * L, :]
        vt = vt_scr[c]
        qk = lax.dot_general(k, q, NT, preferred_element_type=F32)
        per_dir = []
        for d in range(2):
            irow, brow, blast = g8[d:d + 1, :], g8[2 + d:3 + d, :], t8[2 + d:3 + d, :]
            per_dir.append((irow, brow, blast))
            logw = jnp.where(mask_f if d == 0 else mask_b, brow + dcols[:, d:d + 1], -jnp.inf)
            a = jnp.max(logw, axis=0, keepdims=True)
            s = qk * jnp.exp(logw - a)
            h_scr[d, c] = jnp.dot(vt, s.astype(BF16), preferred_element_type=F32)
            dsum = jnp.sum(s, axis=0, keepdims=True)
            grow = blast - brow + irow
            gmax = jnp.max(grow, axis=1, keepdims=True)
            wk = jnp.exp(grow - gmax)
            u_scr[d, c] = jnp.dot(vt * wk.astype(BF16), k, preferred_element_type=F32)
            un = jnp.dot(jnp.broadcast_to(wk, (8, L)).astype(BF16), k, preferred_element_type=F32)
            local[d][c] = (a, dsum, gmax, un)
        rows_of.append(per_dir)

    order = [list(range(nc)), list(range(ncc - 1, -1, -1)) + list(range(nc - 1, ncc - 1, -1))]
    for d in range(2):
        ct = jnp.zeros((MLSTM_V_DIM, MLSTM_QK_DIM), F32)
        n = jnp.zeros((8, MLSTM_QK_DIM), F32)
        m = jnp.zeros((1, 1), F32)
        for c in order[d]:
            irow, brow, blast = rows_of[c][d]
            a, dsum, gmax, un = local[d][c]
            q = qs_scr[c * L:(c + 1) * L, :]
            inter = brow + m
            m_t = jnp.maximum(inter, a)
            e_inter = jnp.exp(inter - m_t)
            e_intra = jnp.exp(a - m_t)
            qc = lax.dot_general(ct.astype(BF16), q, NT, preferred_element_type=F32)
            qn = lax.dot_general(n.astype(BF16), q, NT, preferred_element_type=F32)[0:1, :]
            numer = e_inter * qc + e_intra * h_scr[d, c]
            denom = e_inter * qn + e_intra * dsum
            h_scr[d, c] = numer * (1.0 / jnp.maximum(jnp.abs(denom), jnp.exp(-m_t)))
            m_new = jnp.maximum(blast + m, gmax)
            decay = jnp.exp(blast + m - m_new)
            gain = jnp.exp(gmax - m_new)
            ct = decay * ct + gain * u_scr[d, c]
            n = decay * n + gain * un
            m = m_new

    hn = hn_ref[...]
    for c in range(nc):
        is_ctx, rows = src_rows(c)
        if is_ctx and not with_ctx_out:
            continue
        o_ref, s_ref, y_ref = (oc_ref, sc_ref, yc_ref) if is_ctx else (ol_ref, sl_ref, yl_ref)
        hsum = h_scr[0, c] + h_scr[1, c]
        ms = jnp.mean(hsum * hsum, axis=0, keepdims=True)
        y = (hsum * lax.rsqrt(ms + EPS)).T * hn
        y = y * jax.nn.sigmoid(o_ref[rows, :].astype(F32))
        y_ref[rows, :] = (y * _silu(s_ref[rows, :].astype(F32))).astype(BF16)


def _mlstm_call(dims, p_lat, g_lat, p_ctx, g_ctx, conv_w, conv_b, i_bias, f_bias, head_norm, with_ctx_out):
    mh = dims.ml_heads
    seq, ctx = dims.seq, dims.ctx
    L = min(ctx, 256)
    qk, dv = MLSTM_QK_DIM, MLSTM_V_DIM
    lat = lambda width, off: pl.BlockSpec((seq, width), lambda b, h: (b, off // width + h))
    cx = lambda width, off: pl.BlockSpec((ctx, width), lambda b, h: (b, off // width + h))
    smem = pl.BlockSpec(memory_space=pltpu.SMEM)
    in_specs = [
        lat(qk, dims.col_qb), lat(qk, dims.col_kb), lat(dv, dims.col_vb), lat(LANES, 0),
        lat(dv, dims.col_ob), lat(dv, dims.col_gb),
        cx(qk, dims.col_qb), cx(qk, dims.col_kb), cx(dv, dims.col_vb), cx(LANES, 0),
        cx(dv, dims.col_ob), cx(dv, dims.col_gb),
        pl.BlockSpec((CONV_WIDTH, qk), lambda b, h: (0, h)),
        pl.BlockSpec((CONV_WIDTH, qk), lambda b, h: (0, mh + h)),
        pl.BlockSpec((1, qk), lambda b, h: (0, h)),
        pl.BlockSpec((1, qk), lambda b, h: (0, mh + h)),
        smem, smem,
        pl.BlockSpec((1, dv), lambda b, h: (0, h)),
    ]
    args = [p_lat, p_lat, p_lat, g_lat, p_lat, p_lat, p_ctx, p_ctx, p_ctx, g_ctx, p_ctx, p_ctx,
            conv_w, conv_w, conv_b, conv_b, i_bias, f_bias, head_norm]
    out_specs = [pl.BlockSpec((seq, dv), lambda b, h: (b, h))]
    out_shape = [jax.ShapeDtypeStruct((dims.n_lat, dims.ml_width), BF16)]
    if with_ctx_out:
        out_specs.append(pl.BlockSpec((ctx, dv), lambda b, h: (b, h)))
        out_shape.append(jax.ShapeDtypeStruct((dims.n_ctx, dims.ml_width), BF16))
    nc = (ctx + seq) // L
    scratch = [
        pltpu.VMEM((ctx + seq, qk), BF16), pltpu.VMEM((ctx + seq, qk), BF16),
        pltpu.VMEM((nc, dv, L), BF16),
        pltpu.VMEM((nc, 8, L), F32),
        pltpu.VMEM((2, nc, dv, L), F32),
        pltpu.VMEM((2, nc, dv, qk), F32),
    ]
    return pl.pallas_call(
        functools.partial(_mlstm_kernel, chunk=L, with_ctx_out=with_ctx_out),
        grid=(dims.batch, mh),
        in_specs=in_specs,
        out_specs=out_specs,
        out_shape=out_shape,
        scratch_shapes=scratch,
        compiler_params=_params("arbitrary", "arbitrary"),
        name="mlstm",
    )(*args)


def _outproj_kernel(ya_ref, ym_ref, w_ref, x_ref, gate_ref, nw_ref, o_ref, *, tm, rows_per_mod, fixed_mod_row):
    i = pl.program_id(0)
    d = o_ref.shape[1]
    ka = ya_ref.shape[1]
    nc = min(d, 512)
    ss = jnp.zeros((tm, 1), F32)
    for n in range(0, d, nc):
        acc = (jnp.dot(ya_ref[...], w_ref[0:ka, n:n + nc], preferred_element_type=F32)
               + jnp.dot(ym_ref[...], w_ref[ka:, n:n + nc], preferred_element_type=F32))
        ss = ss + jnp.sum(acc * acc, axis=1, keepdims=True)
        o_ref[:, n:n + nc] = acc
    row = fixed_mod_row if fixed_mod_row is not None else (i * tm) // rows_per_mod
    gate = gate_ref[pl.ds(row, 1), :]
    inv = lax.rsqrt(ss * (1.0 / d) + EPS)
    o_ref[...] = x_ref[...] + gate * (o_ref[...] * inv * nw_ref[...])


def _outproj_call(dims, y_att, y_ml, w_out, layer, x_flat, mod, nw, is_ctx):
    m, d = x_flat.shape
    tm = min(m, 512)
    kernel = functools.partial(_outproj_kernel, tm=tm, rows_per_mod=dims.seq,
                               fixed_mod_row=dims.batch if is_ctx else None)
    return pl.pallas_call(
        kernel,
        grid=(m // tm,),
        in_specs=[
            pl.BlockSpec((tm, y_att.shape[1]), lambda i: (i, 0)),
            pl.BlockSpec((tm, y_ml.shape[1]), lambda i: (i, 0)),
            pl.BlockSpec((None, d, d), lambda i: (layer, 0, 0)),
            pl.BlockSpec((tm, d), lambda i: (i, 0)),
            pl.BlockSpec((MOD_ROWS, d), lambda i: (0, 2)),
            pl.BlockSpec((1, d), lambda i: (0, 0)),
        ],
        out_specs=pl.BlockSpec((tm, d), lambda i: (i, 0)),
        out_shape=jax.ShapeDtypeStruct((m, d), F32),
        compiler_params=_params("arbitrary"),
        name="out_proj_ctx" if is_ctx else "out_proj",
    )(y_att, y_ml, w_out, x_flat, mod, nw)


def _rope_tables(seq):
    rows = seq // GRID_W
    row = jnp.repeat(jnp.arange(rows), GRID_W).astype(F32)
    col = jnp.tile(jnp.arange(GRID_W), rows).astype(F32)
    n_freq = ATT_QK_DIM // 4
    inv = ROPE_BASE ** (-jnp.arange(n_freq, dtype=F32) / n_freq)
    ang = jnp.concatenate([row[:, None] * inv, col[:, None] * inv], axis=-1)
    cos, sin = jnp.cos(ang), jnp.sin(ang)
    return jnp.tile(cos, (1, 4)), jnp.concatenate([-sin, -sin, sin, sin], axis=-1)


def _forward(dims, x, c, ctx, c_ctx, w_ada, b_ada, norm_pre, norm_post, w_in, w_out,
             lam_q1, lam_k1, lam_q2, lam_k2, attn_subln, conv_w, conv_b, i_bias, f_bias, mlstm_norm):
    depth = w_ada.shape[0]
    d = dims.d_model
    x_flat = x.reshape(dims.n_lat, d)
    xc_flat = ctx.reshape(dims.n_ctx, d)
    c_all = jnp.concatenate(
        [c, c_ctx[None], jnp.zeros((MOD_ROWS - dims.batch - 1, d), F32)], axis=0)
    mod_all = _ada_call(c_all, w_ada, b_ada)
    w_main, w_gate = _wprep_call(dims, w_in)
    w_out_b = _cast_call(w_out)
    cos_t, sin_t = _rope_tables(dims.seq)
    for l in range(depth):
        need_ctx = l < depth - 1
        mod = mod_all[l]
        nw_pre = norm_pre[l].reshape(1, d)
        p_lat, g_lat = _inproj_call(dims, x_flat, mod, nw_pre, w_main, l, w_gate, cos_t, sin_t, False)
        p_ctx, g_ctx = _inproj_call(dims, xc_flat, mod, nw_pre, w_main, l, w_gate, cos_t, sin_t, True)
        lam_rows = jnp.stack([lam_q1[l], lam_k1[l], lam_q2[l], lam_k2[l]])
        subln = attn_subln[l].reshape(1, ATT_V_DIM)
        ya_lat = _attn_call(dims, p_lat, p_ctx, lam_rows, subln, l, False)
        ym = _mlstm_call(dims, p_lat, g_lat, p_ctx, g_ctx, conv_w[l], conv_b[l].reshape(1, -1),
                         i_bias[l], f_bias[l], mlstm_norm[l].reshape(1, -1), need_ctx)
        nw_post = norm_post[l].reshape(1, d)
        x_new = _outproj_call(dims, ya_lat, ym[0], w_out_b, l, x_flat, mod, nw_post, False)
        if need_ctx:
            ya_ctx = _attn_call(dims, p_lat, p_ctx, lam_rows, subln, l, True)
            xc_flat = _outproj_call(dims, ya_ctx, ym[1], w_out_b, l, xc_flat, mod, nw_post, True)
        x_flat = x_new
    return x_flat.reshape(x.shape)


def kernel(x, c, ctx, c_ctx, w_ada, b_ada, norm_pre, norm_post, w_in, w_out, lam_q1, lam_k1, lam_q2, lam_k2,
           attn_subln, conv_w, conv_b, i_bias, f_bias, mlstm_norm):
    dims = Dims(batch=x.shape[0], seq=x.shape[1], ctx=ctx.shape[1], d_model=x.shape[2])
    return _forward(dims, x, c, ctx, c_ctx, w_ada, b_ada, norm_pre, norm_post, w_in, w_out,
                    lam_q1, lam_k1, lam_q2, lam_k2, attn_subln, conv_w, conv_b, i_bias, f_bias, mlstm_norm)
```

```python
import functools
import math
from typing import NamedTuple

import numpy as np
import jax
import jax.numpy as jnp
from jax import lax
from jax.experimental import pallas as pl
from jax.experimental.pallas import tpu as pltpu

F32 = jnp.float32
BF16 = jnp.bfloat16

LANES = 128
ATT_V_DIM = 128
ATT_QK_DIM = 64
MLSTM_V_DIM = 256
MLSTM_QK_DIM = 128
CONV_WIDTH = 3
GATE_SOFTCAP = 15.0
ROPE_BASE = 10000.0
GRID_W = 64
EPS = 1e-6
MOD_ROWS = 16
VMEM_LIMIT = 56 * 1024 * 1024
NT = (((1,), (1,)), ((), ()))
MIN_SOFTMAX_DENOMINATOR = 2.0 ** -60


class Dims(NamedTuple):
    batch: int
    seq: int
    ctx: int
    d_model: int

    @property
    def att_width(self):
        return self.d_model // 2

    @property
    def att_heads(self):
        return self.att_width // ATT_V_DIM

    @property
    def ml_width(self):
        return self.d_model - self.att_width

    @property
    def ml_heads(self):
        return self.ml_width // MLSTM_V_DIM

    @property
    def n_lat(self):
        return self.batch * self.seq

    @property
    def n_ctx(self):
        return self.batch * self.ctx

    @property
    def col_qa(self):
        return 0

    @property
    def col_ka(self):
        return self.att_width

    @property
    def col_va(self):
        return 2 * self.att_width

    @property
    def col_ga(self):
        return 3 * self.att_width

    @property
    def col_qb(self):
        return 4 * self.att_width

    @property
    def col_kb(self):
        return self.col_qb + self.ml_heads * MLSTM_QK_DIM

    @property
    def col_vb(self):
        return self.col_kb + self.ml_heads * MLSTM_QK_DIM

    @property
    def col_ob(self):
        return self.col_vb + self.ml_width

    @property
    def col_gb(self):
        return self.col_ob + self.ml_width

    @property
    def main_cols(self):
        return self.col_gb + self.ml_width


def _params(*sem):
    return pltpu.CompilerParams(dimension_semantics=sem, vmem_limit_bytes=VMEM_LIMIT)


def _silu(x):
    return x * jax.nn.sigmoid(x)


def _wprep_kernel(w_ref, wg_ref, perm_ref, place_ref, o_ref, og_ref, *, qk_tiles, q_tiles, q_scale):
    j = pl.program_id(1)

    @pl.when(j < qk_tiles)
    def _():
        x = (w_ref[0] * jnp.where(j < q_tiles, q_scale, 1.0)).astype(BF16)
        for r in range(0, x.shape[0], LANES):
            o_ref[0, r:r + LANES, :] = jnp.dot(
                perm_ref[...], x[r:r + LANES, :], preferred_element_type=F32).astype(BF16)

    @pl.when(j >= qk_tiles)
    def _():
        o_ref[0] = w_ref[0].astype(BF16)

    @pl.when(j == 0)
    def _():
        og_ref[0] = jnp.dot(place_ref[...], wg_ref[0].astype(BF16), preferred_element_type=F32).astype(BF16)


def _rope_row_permutation():
    perm = np.zeros((LANES, LANES), np.float32)
    for c in range(2):
        for i in range(ATT_QK_DIM // 2):
            for par in range(2):
                perm[par * ATT_QK_DIM + c * (ATT_QK_DIM // 2) + i, c * ATT_QK_DIM + 2 * i + par] = 1.0
    return jnp.asarray(perm, BF16)


def _gate_row_placement(dims):
    mh = dims.ml_heads
    place = np.zeros((mh * LANES, 4 * mh), np.float32)
    for kind in range(4):
        for h in range(mh):
            place[h * LANES + kind, kind * mh + h] = 1.0
    return jnp.asarray(place, BF16)


def _wprep_call(dims, w_in):
    depth, d, n_all = w_in.shape
    mh = dims.ml_heads
    tn = min(dims.att_width, 1024)
    n_gate = n_all - dims.main_cols
    assert dims.att_width % tn == 0 and dims.main_cols % tn == 0
    assert dims.main_cols % n_gate == 0 and n_gate == 4 * mh and n_gate % 8 == 0
    kernel = functools.partial(_wprep_kernel, qk_tiles=2 * dims.att_width // tn, q_tiles=dims.att_width // tn,
                               q_scale=ATT_QK_DIM ** -0.5 * math.log2(math.e))
    w_t = jnp.swapaxes(w_in, 1, 2)
    return pl.pallas_call(
        kernel,
        grid=(depth, dims.main_cols // tn),
        in_specs=[
            pl.BlockSpec((1, tn, d), lambda l, j: (l, j, 0)),
            pl.BlockSpec((1, n_gate, d), lambda l, j: (l, dims.main_cols // n_gate, 0)),
            pl.BlockSpec((LANES, LANES), lambda l, j: (0, 0)),
            pl.BlockSpec((mh * LANES, n_gate), lambda l, j: (0, 0)),
        ],
        out_specs=[
            pl.BlockSpec((1, tn, d), lambda l, j: (l, j, 0)),
            pl.BlockSpec((1, mh * LANES, d), lambda l, j: (l, 0, 0)),
        ],
        out_shape=[
            jax.ShapeDtypeStruct((depth, dims.main_cols, d), BF16),
            jax.ShapeDtypeStruct((depth, mh * LANES, d), BF16),
        ],
        compiler_params=_params("arbitrary", "arbitrary"),
        name="w_prep",
    )(w_t, w_t, _rope_row_permutation(), _gate_row_placement(dims))


def _cast_kernel(w_ref, o_ref):
    o_ref[...] = w_ref[...].astype(BF16)


def _cast_call(w):
    depth, r, c = w.shape
    tr = min(r, 512)
    return pl.pallas_call(
        _cast_kernel,
        grid=(depth, r // tr),
        in_specs=[pl.BlockSpec((1, tr, c), lambda l, i: (l, i, 0))],
        out_specs=pl.BlockSpec((1, tr, c), lambda l, i: (l, i, 0)),
        out_shape=jax.ShapeDtypeStruct(w.shape, BF16),
        compiler_params=_params("arbitrary", "arbitrary"),
        name="w_cast",
    )(w)


def _ada_kernel(c_ref, w_ref, b_ref, o_ref):
    s = _silu(c_ref[...]).astype(BF16)
    w = w_ref[0].astype(BF16)
    o_ref[0] = jnp.dot(s, w, preferred_element_type=F32) + b_ref[0]


def _ada_call(c_all, w_ada, b_ada):
    depth, d, n = w_ada.shape
    tn = min(n, 1024)
    return pl.pallas_call(
        _ada_kernel,
        grid=(depth, n // tn),
        in_specs=[
            pl.BlockSpec((MOD_ROWS, d), lambda l, j: (0, 0)),
            pl.BlockSpec((1, d, tn), lambda l, j: (l, 0, j)),
            pl.BlockSpec((1, 1, tn), lambda l, j: (l, 0, j)),
        ],
        out_specs=pl.BlockSpec((1, MOD_ROWS, tn), lambda l, j: (l, 0, j)),
        out_shape=jax.ShapeDtypeStruct((depth, MOD_ROWS, n), F32),
        compiler_params=_params("arbitrary", "arbitrary"),
        name="ada",
    )(c_all, w_ada, b_ada.reshape(depth, 1, n))


def _inproj_kernel(x_ref, shift_ref, scale_ref, nw_ref, w_ref, wg_ref, cos_ref, sin_ref, p_ref, g_ref, h_scr,
                   *, tm, tn, rows_per_mod, fixed_mod_row, rope_tiles):
    i = pl.program_id(0)
    j = pl.program_id(1)
    rc = min(tm, 256)

    @pl.when(j == 0)
    def _():
        row = fixed_mod_row if fixed_mod_row is not None else (i * tm) // rows_per_mod
        shift = shift_ref[pl.ds(row, 1), :]
        scale1 = 1.0 + scale_ref[pl.ds(row, 1), :]
        nw = nw_ref[...]
        for r in range(0, tm, rc):
            xf = x_ref[r:r + rc, :]
            ms = jnp.mean(xf * xf, axis=-1, keepdims=True)
            y = xf * lax.rsqrt(ms + EPS) * nw
            h_scr[r:r + rc, :] = (y * scale1 + shift).astype(BF16)
        g_ref[...] = lax.dot_general(h_scr[...], wg_ref[...], NT, preferred_element_type=F32)

    def project(rope):
        for n in range(0, tn, 2 * LANES):
            acc = lax.dot_general(h_scr[...], w_ref[n:n + 2 * LANES, :], NT, preferred_element_type=F32)
            if rope:
                for hh in range(2):
                    a = acc[:, hh * LANES:(hh + 1) * LANES]
                    r = a * cos_ref[...] + pltpu.roll(a, LANES // 2, axis=1) * sin_ref[...]
                    p_ref[:, n + hh * LANES:n + (hh + 1) * LANES] = r.astype(BF16)
            else:
                p_ref[:, n:n + 2 * LANES] = acc.astype(BF16)

    if rope_tiles:
        pl.when(j < rope_tiles)(lambda: project(True))
        pl.when(j >= rope_tiles)(lambda: project(False))
    else:
        project(False)


def _inproj_call(dims, x_flat, mod, nw, w_main, layer, w_gate, cos_t, sin_t, is_ctx):
    m, d = x_flat.shape
    n_main = w_main.shape[1]
    n_gate = w_gate.shape[1]
    tm = min(m if is_ctx else dims.seq, 1024)
    tn = min(n_main, 1024)
    assert (2 * dims.att_width) % tn == 0
    kernel = functools.partial(
        _inproj_kernel, tm=tm, tn=tn, rows_per_mod=dims.seq,
        fixed_mod_row=dims.batch if is_ctx else None,
        rope_tiles=0 if is_ctx else (2 * dims.att_width) // tn)
    tiles_per_seq = max(dims.seq // tm, 1)
    return pl.pallas_call(
        kernel,
        grid=(m // tm, n_main // tn),
        in_specs=[
            pl.BlockSpec((tm, d), lambda i, j: (i, 0)),
            pl.BlockSpec((MOD_ROWS, d), lambda i, j: (0, 0)),
            pl.BlockSpec((MOD_ROWS, d), lambda i, j: (0, 1)),
            pl.BlockSpec((1, d), lambda i, j: (0, 0)),
            pl.BlockSpec((None, tn, d), lambda i, j: (layer, j, 0)),
            pl.BlockSpec((None, n_gate, d), lambda i, j: (layer, 0, 0)),
            pl.BlockSpec((tm, LANES), lambda i, j: (i % tiles_per_seq, 0)),
            pl.BlockSpec((tm, LANES), lambda i, j: (i % tiles_per_seq, 0)),
        ],
        out_specs=[
            pl.BlockSpec((tm, tn), lambda i, j: (i, j)),
            pl.BlockSpec((tm, n_gate), lambda i, j: (i, 0)),
        ],
        out_shape=[
            jax.ShapeDtypeStruct((m, n_main), BF16),
            jax.ShapeDtypeStruct((m, n_gate), F32),
        ],
        scratch_shapes=[pltpu.VMEM((tm, d), BF16)],
        compiler_params=_params("arbitrary", "arbitrary"),
        name="in_proj_ctx" if is_ctx else "in_proj",
    )(x_flat, mod, mod, nw, w_main, w_gate, cos_t, sin_t)


def _lambda(lam_ref, lam_init):
    lp = lam_ref[...]
    return (jnp.exp(jnp.sum(lp[0:1] * lp[1:2], axis=1, keepdims=True))
            - jnp.exp(jnp.sum(lp[2:3] * lp[3:4], axis=1, keepdims=True)) + lam_init)


def _first_sub_head_lanes():
    lane = lax.broadcasted_iota(jnp.int32, (1, LANES), 1)
    return (lane % (LANES // 2)) < (LANES // 4)


def _attn_finish(o_t, lam_init, sub_ref, g):
    ms = jnp.mean(o_t * o_t, axis=0, keepdims=True)
    y = (o_t * lax.rsqrt(ms + EPS)).T
    y = y * sub_ref[...] * (1.0 - lam_init)
    return (y * _silu(g.astype(F32))).astype(BF16)


def _attn_lat_kernel(q_ref, kc_ref, vc_ref, kl_ref, vl_ref, g_ref, lam_ref, sub_ref, o_ref,
                     vt_scr, s_scr, redo_scr, *, lam_init, tq, kc):
    ctx, seq = kc_ref.shape[0], kl_ref.shape[0]
    n_heads = q_ref.shape[1] // LANES
    n_tiles = seq // tq
    n_kc = (ctx + seq) // kc
    head_cols = [slice(h * LANES, (h + 1) * LANES) for h in range(n_heads)]

    for h, cols in enumerate(head_cols):
        for r in range(0, ctx, kc):
            vt_scr[h, :, r:r + kc] = vc_ref[r:r + kc, cols].T
        for r in range(0, seq, kc):
            vt_scr[h, :, ctx + r:ctx + r + kc] = vl_ref[r:r + kc, cols].T

    lam = _lambda(lam_ref, lam_init)
    first = _first_sub_head_lanes()

    row8 = lax.broadcasted_iota(jnp.int32, (8, LANES), 0)
    pick = jnp.where(jnp.logical_or(jnp.logical_and(row8 == 0, first),
                                    jnp.logical_and(row8 == 1, jnp.logical_not(first))), 1.0, 0.0).astype(BF16)

    def sq_norms(x):
        return lax.dot_general(pick, x * x, NT, preferred_element_type=F32)

    k_max2 = [jnp.maximum(jnp.max(sq_norms(kc_ref[:, cols]), axis=1, keepdims=True),
                          jnp.max(sq_norms(kl_ref[:, cols]), axis=1, keepdims=True)) for cols in head_cols]

    def sub_head(q, sub):
        return jnp.where(first if sub == 0 else jnp.logical_not(first), q, jnp.zeros_like(q))

    def scores(h, qs, slot):
        s_c = lax.dot_general(kc_ref[:, head_cols[h]], qs, NT, preferred_element_type=F32)
        s_l = lax.dot_general(kl_ref[:, head_cols[h]], qs, NT, preferred_element_type=F32)
        s_scr[slot, 0:ctx] = s_c
        s_scr[slot, ctx:] = s_l
        return s_c, s_l

    def softmax_values(h, slot, m):
        den = jnp.zeros((1, tq), F32)
        o_t = jnp.zeros((ATT_V_DIM, tq), F32)
        for i in range(n_kc):
            p = jnp.exp2(s_scr[slot, i * kc:(i + 1) * kc, :] - m)
            den = den + jnp.sum(p, axis=0, keepdims=True)
            o_t = o_t + jnp.dot(vt_scr[h, :, i * kc:(i + 1) * kc], p.astype(BF16), preferred_element_type=F32)
        return o_t * (1.0 / den), den

    o_first = None
    den_min = None
    unit = 0
    for h, cols in enumerate(head_cols):
        for t in range(n_tiles):
            rows = slice(t * tq, (t + 1) * tq)
            q = q_ref[rows, cols]
            q_norm2 = sq_norms(q)
            for sub in range(2):
                bound = jnp.sqrt(q_norm2[sub:sub + 1, :] * k_max2[h][sub:sub + 1, :]) * 1.05
                scores(h, sub_head(q, sub), unit % 2)
                o_t, den = softmax_values(h, unit % 2, bound)
                den_min = den if den_min is None else jnp.minimum(den_min, den)
                unit += 1
                if sub == 0:
                    o_first = o_t
                else:
                    o_ref[rows, cols] = _attn_finish(o_first - lam * o_t, lam_init, sub_ref, g_ref[rows, cols])

    ok = jnp.min(den_min, axis=1, keepdims=True) >= MIN_SOFTMAX_DENOMINATOR
    redo_scr[0] = jnp.where(ok, 0, 1)[0, 0]

    @pl.when(redo_scr[0] != 0)
    def _():
        for h, cols in enumerate(head_cols):
            def exact_tile(t, carry, h=h, cols=cols):
                rows = pl.ds(pl.multiple_of(t * tq, tq), tq)
                q = q_ref[rows, cols]
                outs = []
                for sub in range(2):
                    s_c, s_l = scores(h, sub_head(q, sub), sub)
                    m = jnp.maximum(jnp.max(s_c, axis=0, keepdims=True), jnp.max(s_l, axis=0, keepdims=True))
                    outs.append(softmax_values(h, sub, m)[0])
                o_ref[rows, cols] = _attn_finish(outs[0] - lam * outs[1], lam_init, sub_ref, g_ref[rows, cols])
                return carry

            lax.fori_loop(0, n_tiles, exact_tile, 0)


def _attn_ctx_kernel(q_ref, kc_ref, vc_ref, g_ref, lam_ref, sub_ref, o_ref, *, lam_init):
    lam = _lambda(lam_ref, lam_init)
    first = _first_sub_head_lanes()
    for h in range(q_ref.shape[1] // LANES):
        cols = slice(h * LANES, (h + 1) * LANES)
        q = q_ref[:, cols]
        k = kc_ref[:, cols]
        vt = vc_ref[:, cols].T
        outs = []
        for sub in range(2):
            qs = jnp.where(first if sub == 0 else jnp.logical_not(first), q, jnp.zeros_like(q))
            s = lax.dot_general(k, qs, NT, preferred_element_type=F32)
            p = jnp.exp2(s - jnp.max(s, axis=0, keepdims=True))
            den = jnp.sum(p, axis=0, keepdims=True)
            outs.append(jnp.dot(vt, p.astype(BF16), preferred_element_type=F32) * (1.0 / den))
        o_ref[:, cols] = _attn_finish(outs[0] - lam * outs[1], lam_init, sub_ref, g_ref[:, cols])


def _attn_call(dims, p_lat, p_ctx, lam_rows, subln, layer_idx, is_ctx):
    lam_init = 0.8 - 0.6 * math.exp(-0.3 * layer_idx)
    ah = dims.att_heads
    seq, ctx = dims.seq, dims.ctx
    small = [pl.BlockSpec((4, ATT_QK_DIM), lambda b, h: (0, 0)),
             pl.BlockSpec((1, ATT_V_DIM), lambda b, h: (0, 0))]
    if is_ctx:
        aw = dims.att_width
        wide = lambda off: pl.BlockSpec((ctx, aw), lambda b: (b, off // aw))
        return pl.pallas_call(
            functools.partial(_attn_ctx_kernel, lam_init=lam_init),
            grid=(dims.batch,),
            in_specs=[wide(dims.col_qa), wide(dims.col_ka), wide(dims.col_va), wide(dims.col_ga),
                      pl.BlockSpec((4, ATT_QK_DIM), lambda b: (0, 0)),
                      pl.BlockSpec((1, ATT_V_DIM), lambda b: (0, 0))],
            out_specs=pl.BlockSpec((ctx, aw), lambda b: (b, 0)),
            out_shape=jax.ShapeDtypeStruct((dims.n_ctx, aw), BF16),
            compiler_params=_params("arbitrary"),
            name="attn_ctx",
        )(p_ctx, p_ctx, p_ctx, p_ctx, lam_rows, subln)
    tq = min(seq, 512)
    kc = min(ctx, 256)
    hg = 2 if ah % 2 == 0 else 1
    w = hg * LANES
    lat_g = lambda off: pl.BlockSpec((seq, w), lambda b, h: (b, off // w + h))
    cx_g = lambda off: pl.BlockSpec((ctx, w), lambda b, h: (b, off // w + h))
    return pl.pallas_call(
        functools.partial(_attn_lat_kernel, lam_init=lam_init, tq=tq, kc=kc),
        grid=(dims.batch, ah // hg),
        in_specs=[lat_g(dims.col_qa), cx_g(dims.col_ka), cx_g(dims.col_va), lat_g(dims.col_ka),
                  lat_g(dims.col_va), lat_g(dims.col_ga)] + small,
        out_specs=pl.BlockSpec((seq, w), lambda b, h: (b, h)),
        out_shape=jax.ShapeDtypeStruct((dims.n_lat, dims.att_width), BF16),
        scratch_shapes=[
            pltpu.VMEM((hg, ATT_V_DIM, ctx + seq), BF16),
            pltpu.VMEM((2, ctx + seq, tq), F32),
            pltpu.SMEM((1,), jnp.int32),
        ],
        compiler_params=_params("arbitrary", "arbitrary"),
        name="attn",
    )(p_lat, p_ctx, p_ctx, p_lat, p_lat, p_lat, lam_rows, subln)


def _split3(x):
    hi = x.astype(BF16)
    r1 = x - hi.astype(F32)
    mid = r1.astype(BF16)
    lo = (r1 - mid.astype(F32)).astype(BF16)
    return hi, mid, lo


def _conv_silu(src_ref, w, b, scale, dst_ref):
    x = src_ref[...].astype(F32)
    t = x.shape[0]
    row = lax.broadcasted_iota(jnp.int32, (t, 1), 0)
    prev = jnp.where(row == 0, 0.0, pltpu.roll(x, 1, axis=0))
    nxt = jnp.where(row == t - 1, 0.0, pltpu.roll(x, t - 1, axis=0))
    y = b + w[0:1] * prev
    y = y + w[1:2] * x
    y = y + w[2:3] * nxt
    y = _silu(y)
    if scale is not None:
        y = y * scale
    dst_ref[...] = y.astype(BF16)


def _mlstm_kernel(*refs, chunk, with_ctx_out):
    (ql_ref, kl_ref, vl_ref, gl_ref, ol_ref, sl_ref,
     qc_ref, kc_ref, vc_ref, gc_ref, oc_ref, sc_ref,
     cwq_ref, cwk_ref, cbq_ref, cbk_ref, ib_ref, fb_ref, hn_ref) = refs[:19]
    refs = refs[19:]
    if with_ctx_out:
        yl_ref, yc_ref = refs[:2]
        refs = refs[2:]
    else:
        yl_ref, yc_ref = refs[0], None
        refs = refs[1:]
    qs_scr, ks_scr, vt_scr, gr_scr, h_scr, u_scr = refs
    L = chunk
    ctx, seq = qc_ref.shape[0], ql_ref.shape[0]
    ncc, nc = ctx // L, (ctx + seq) // L
    hd = pl.program_id(1)

    k_scale = MLSTM_QK_DIM ** -0.5
    _conv_silu(qc_ref, cwq_ref[...], cbq_ref[...], None, qs_scr.at[0:ctx])
    _conv_silu(kc_ref, cwk_ref[...], cbk_ref[...], k_scale, ks_scr.at[0:ctx])
    _conv_silu(ql_ref, cwq_ref[...], cbq_ref[...], None, qs_scr.at[ctx:])
    _conv_silu(kl_ref, cwk_ref[...], cbk_ref[...], k_scale, ks_scr.at[ctx:])

    def src_rows(c):
        return (True, slice(c * L, (c + 1) * L)) if c < ncc else (False, slice((c - ncc) * L, (c - ncc + 1) * L))

    for c in range(nc):
        is_ctx, rows = src_rows(c)
        vt_scr[c] = (vc_ref if is_ctx else vl_ref)[rows, :].T
        gr_scr[c] = (gc_ref if is_ctx else gl_ref)[rows, :].T[0:8, :]

    row = lax.broadcasted_iota(jnp.int32, (1, 8, 1), 1)
    bias = jnp.where(row == 0, ib_ref[0, hd],
                     jnp.where(row == 1, ib_ref[1, hd],
                               jnp.where(row == 2, fb_ref[0, hd],
                                         jnp.where(row == 3, fb_ref[1, hd], 0.0))))
    capped = GATE_SOFTCAP * jnp.tanh((gr_scr[...] + bias) * (1.0 / GATE_SOFTCAP))
    lsig = jnp.minimum(capped, 0.0) - jnp.log1p(jnp.exp(-jnp.abs(capped)))
    lf = jnp.where((row >= 2) & (row < 4), lsig, 0.0).reshape(nc * 8, L)
    ui = lax.broadcasted_iota(jnp.int32, (L, L), 0)
    ti = lax.broadcasted_iota(jnp.int32, (L, L), 1)
    tri_t = (ui <= ti).astype(BF16)
    hi, mid, lo = _split3(lf)
    prefix = (jnp.dot(hi, tri_t, preferred_element_type=F32) + jnp.dot(mid, tri_t, preferred_element_type=F32)
              + jnp.dot(lo, tri_t, preferred_element_type=F32))
    tot = prefix[:, L - 1:L]
    suffix = tot - prefix + lf
    row2 = lax.broadcasted_iota(jnp.int32, (nc * 8, 1), 0) % 8
    gates = jnp.where(row2 < 2, capped.reshape(nc * 8, L), jnp.where(row2 == 2, prefix, suffix))

    mask_f = ui <= ti
    mask_b = ui >= ti
    zeros_pad = jnp.zeros((LANES - 8, L), F32)

    rows_of = []
    local = [[None] * nc for _ in range(2)]
    for c in range(nc):
        g8 = gates[c * 8:(c + 1) * 8, :]
        t8 = tot[c * 8:(c + 1) * 8, :]
        dcols = jnp.concatenate([g8 - pltpu.roll(g8, 6, axis=0), zeros_pad], axis=0).T
        q = qs_scr[c * L:(c + 1) * L, :]
        k = ks_scr[c * L:(c + 1) * L, :]
        vt = vt_scr[c]
        qk = lax.dot_general(k, q, NT, preferred_element_type=F32)
        per_dir = []
        for d in range(2):
            irow, brow, blast = g8[d:d + 1, :], g8[2 + d:3 + d, :], t8[2 + d:3 + d, :]
            per_dir.append((irow, brow, blast))
            logw = jnp.where(mask_f if d == 0 else mask_b, brow + dcols[:, d:d + 1], -jnp.inf)
            a = jnp.max(logw, axis=0, keepdims=True)
            s = qk * jnp.exp(logw - a)
            h_scr[d, c] = jnp.dot(vt, s.astype(BF16), preferred_element_type=F32)
            dsum = jnp.sum(s, axis=0, keepdims=True)
            grow = blast - brow + irow
            gmax = jnp.max(grow, axis=1, keepdims=True)
            wk = jnp.exp(grow - gmax)
            u_scr[d, c] = jnp.dot(vt * wk.astype(BF16), k, preferred_element_type=F32)
            un = jnp.dot(jnp.broadcast_to(wk, (8, L)).astype(BF16), k, preferred_element_type=F32)
            local[d][c] = (a, dsum, gmax, un)
        rows_of.append(per_dir)

    order = [list(range(nc)), list(range(ncc - 1, -1, -1)) + list(range(nc - 1, ncc - 1, -1))]
    for d in range(2):
        ct = jnp.zeros((MLSTM_V_DIM, MLSTM_QK_DIM), F32)
        n = jnp.zeros((8, MLSTM_QK_DIM), F32)
        m = jnp.zeros((1, 1), F32)
        for c in order[d]:
            irow, brow, blast = rows_of[c][d]
            a, dsum, gmax, un = local[d][c]
            q = qs_scr[c * L:(c + 1) * L, :]
            inter = brow + m
            m_t = jnp.maximum(inter, a)
            e_inter = jnp.exp(inter - m_t)
            e_intra = jnp.exp(a - m_t)
            qc = lax.dot_general(ct.astype(BF16), q, NT, preferred_element_type=F32)
            qn = lax.dot_general(n.astype(BF16), q, NT, preferred_element_type=F32)[0:1, :]
            numer = e_inter * qc + e_intra * h_scr[d, c]
            denom = e_inter * qn + e_intra * dsum
            h_scr[d, c] = numer * (1.0 / jnp.maximum(jnp.abs(denom), jnp.exp(-m_t)))
            m_new = jnp.maximum(blast + m, gmax)
            decay = jnp.exp(blast + m - m_new)
            gain = jnp.exp(gmax - m_new)
            ct = decay * ct + gain * u_scr[d, c]
            n = decay * n + gain * un
            m = m_new

    hn = hn_ref[...]
    for c in range(nc):
        is_ctx, rows = src_rows(c)
        if is_ctx and not with_ctx_out:
            continue
        o_ref, s_ref, y_ref = (oc_ref, sc_ref, yc_ref) if is_ctx else (ol_ref, sl_ref, yl_ref)
        hsum = h_scr[0, c] + h_scr[1, c]
        ms = jnp.mean(hsum * hsum, axis=0, keepdims=True)
        y = (hsum * lax.rsqrt(ms + EPS)).T * hn
        y = y * jax.nn.sigmoid(o_ref[rows, :].astype(F32))
        y_ref[rows, :] = (y * _silu(s_ref[rows, :].astype(F32))).astype(BF16)


def _mlstm_call(dims, p_lat, g_lat, p_ctx, g_ctx, conv_w, conv_b, i_bias, f_bias, head_norm, with_ctx_out):
    mh = dims.ml_heads
    seq, ctx = dims.seq, dims.ctx
    L = min(ctx, 256)
    qk, dv = MLSTM_QK_DIM, MLSTM_V_DIM
    lat = lambda width, off: pl.BlockSpec((seq, width), lambda b, h: (b, off // width + h))
    cx = lambda width, off: pl.BlockSpec((ctx, width), lambda b, h: (b, off // width + h))
    smem = pl.BlockSpec(memory_space=pltpu.SMEM)
    in_specs = [
        lat(qk, dims.col_qb), lat(qk, dims.col_kb), lat(dv, dims.col_vb), lat(LANES, 0),
        lat(dv, dims.col_ob), lat(dv, dims.col_gb),
        cx(qk, dims.col_qb), cx(qk, dims.col_kb), cx(dv, dims.col_vb), cx(LANES, 0),
        cx(dv, dims.col_ob), cx(dv, dims.col_gb),
        pl.BlockSpec((CONV_WIDTH, qk), lambda b, h: (0, h)),
        pl.BlockSpec((CONV_WIDTH, qk), lambda b, h: (0, mh + h)),
        pl.BlockSpec((1, qk), lambda b, h: (0, h)),
        pl.BlockSpec((1, qk), lambda b, h: (0, mh + h)),
        smem, smem,
        pl.BlockSpec((1, dv), lambda b, h: (0, h)),
    ]
    args = [p_lat, p_lat, p_lat, g_lat, p_lat, p_lat, p_ctx, p_ctx, p_ctx, g_ctx, p_ctx, p_ctx,
            conv_w, conv_w, conv_b, conv_b, i_bias, f_bias, head_norm]
    out_specs = [pl.BlockSpec((seq, dv), lambda b, h: (b, h))]
    out_shape = [jax.ShapeDtypeStruct((dims.n_lat, dims.ml_width), BF16)]
    if with_ctx_out:
        out_specs.append(pl.BlockSpec((ctx, dv), lambda b, h: (b, h)))
        out_shape.append(jax.ShapeDtypeStruct((dims.n_ctx, dims.ml_width), BF16))
    nc = (ctx + seq) // L
    scratch = [
        pltpu.VMEM((ctx + seq, qk), BF16), pltpu.VMEM((ctx + seq, qk), BF16),
        pltpu.VMEM((nc, dv, L), BF16),
        pltpu.VMEM((nc, 8, L), F32),
        pltpu.VMEM((2, nc, dv, L), F32),
        pltpu.VMEM((2, nc, dv, qk), F32),
    ]
    return pl.pallas_call(
        functools.partial(_mlstm_kernel, chunk=L, with_ctx_out=with_ctx_out),
        grid=(dims.batch, mh),
        in_specs=in_specs,
        out_specs=out_specs,
        out_shape=out_shape,
        scratch_shapes=scratch,
        compiler_params=_params("arbitrary", "arbitrary"),
        name="mlstm",
    )(*args)


def _outproj_kernel(ya_ref, ym_ref, w_ref, x_ref, gate_ref, nw_ref, o_ref, *, tm, rows_per_mod, fixed_mod_row):
    i = pl.program_id(0)
    d = o_ref.shape[1]
    ka = ya_ref.shape[1]
    nc = min(d, 512)
    ss = jnp.zeros((tm, 1), F32)
    for n in range(0, d, nc):
        acc = (jnp.dot(ya_ref[...], w_ref[0:ka, n:n + nc], preferred_element_type=F32)
               + jnp.dot(ym_ref[...], w_ref[ka:, n:n + nc], preferred_element_type=F32))
        ss = ss + jnp.sum(acc * acc, axis=1, keepdims=True)
        o_ref[:, n:n + nc] = acc
    row = fixed_mod_row if fixed_mod_row is not None else (i * tm) // rows_per_mod
    gate = gate_ref[pl.ds(row, 1), :]
    inv = lax.rsqrt(ss * (1.0 / d) + EPS)
    o_ref[...] = x_ref[...] + gate * (o_ref[...] * inv * nw_ref[...])


def _outproj_call(dims, y_att, y_ml, w_out, layer, x_flat, mod, nw, is_ctx):
    m, d = x_flat.shape
    tm = min(m, 512)
    kernel = functools.partial(_outproj_kernel, tm=tm, rows_per_mod=dims.seq,
                               fixed_mod_row=dims.batch if is_ctx else None)
    return pl.pallas_call(
        kernel,
        grid=(m // tm,),
        in_specs=[
            pl.BlockSpec((tm, y_att.shape[1]), lambda i: (i, 0)),
            pl.BlockSpec((tm, y_ml.shape[1]), lambda i: (i, 0)),
            pl.BlockSpec((None, d, d), lambda i: (layer, 0, 0)),
            pl.BlockSpec((tm, d), lambda i: (i, 0)),
            pl.BlockSpec((MOD_ROWS, d), lambda i: (0, 2)),
            pl.BlockSpec((1, d), lambda i: (0, 0)),
        ],
        out_specs=pl.BlockSpec((tm, d), lambda i: (i, 0)),
        out_shape=jax.ShapeDtypeStruct((m, d), F32),
        compiler_params=_params("arbitrary"),
        name="out_proj_ctx" if is_ctx else "out_proj",
    )(y_att, y_ml, w_out, x_flat, mod, nw)


def _rope_tables(seq):
    rows = seq // GRID_W
    row = jnp.repeat(jnp.arange(rows), GRID_W).astype(F32)
    col = jnp.tile(jnp.arange(GRID_W), rows).astype(F32)
    n_freq = ATT_QK_DIM // 4
    inv = ROPE_BASE ** (-jnp.arange(n_freq, dtype=F32) / n_freq)
    ang = jnp.concatenate([row[:, None] * inv, col[:, None] * inv], axis=-1)
    cos, sin = jnp.cos(ang), jnp.sin(ang)
    return jnp.tile(cos, (1, 4)), jnp.concatenate([-sin, -sin, sin, sin], axis=-1)


def _forward(dims, x, c, ctx, c_ctx, w_ada, b_ada, norm_pre, norm_post, w_in, w_out,
             lam_q1, lam_k1, lam_q2, lam_k2, attn_subln, conv_w, conv_b, i_bias, f_bias, mlstm_norm):
    depth = w_ada.shape[0]
    d = dims.d_model
    x_flat = x.reshape(dims.n_lat, d)
    xc_flat = ctx.reshape(dims.n_ctx, d)
    c_all = jnp.concatenate(
        [c, c_ctx[None], jnp.zeros((MOD_ROWS - dims.batch - 1, d), F32)], axis=0)
    mod_all = _ada_call(c_all, w_ada, b_ada)
    w_main, w_gate = _wprep_call(dims, w_in)
    w_out_b = _cast_call(w_out)
    cos_t, sin_t = _rope_tables(dims.seq)
    for l in range(depth):
        need_ctx = l < depth - 1
        mod = mod_all[l]
        nw_pre = norm_pre[l].reshape(1, d)
        p_lat, g_lat = _inproj_call(dims, x_flat, mod, nw_pre, w_main, l, w_gate, cos_t, sin_t, False)
        p_ctx, g_ctx = _inproj_call(dims, xc_flat, mod, nw_pre, w_main, l, w_gate, cos_t, sin_t, True)
        lam_rows = jnp.stack([lam_q1[l], lam_k1[l], lam_q2[l], lam_k2[l]])
        subln = attn_subln[l].reshape(1, ATT_V_DIM)
        ya_lat = _attn_call(dims, p_lat, p_ctx, lam_rows, subln, l, False)
        ym = _mlstm_call(dims, p_lat, g_lat, p_ctx, g_ctx, conv_w[l], conv_b[l].reshape(1, -1),
                         i_bias[l], f_bias[l], mlstm_norm[l].reshape(1, -1), need_ctx)
        nw_post = norm_post[l].reshape(1, d)
        x_new = _outproj_call(dims, ya_lat, ym[0], w_out_b, l, x_flat, mod, nw_post, False)
        if need_ctx:
            ya_ctx = _attn_call(dims, p_lat, p_ctx, lam_rows, subln, l, True)
            xc_flat = _outproj_call(dims, ya_ctx, ym[1], w_out_b, l, xc_flat, mod, nw_post, True)
        x_flat = x_new
    return x_flat.reshape(x.shape)


def kernel(x, c, ctx, c_ctx, w_ada, b_ada, norm_pre, norm_post, w_in, w_out, lam_q1, lam_k1, lam_q2, lam_k2,
           attn_subln, conv_w, conv_b, i_bias, f_bias, mlstm_norm):
    dims = Dims(batch=x.shape[0], seq=x.shape[1], ctx=ctx.shape[1], d_model=x.shape[2])
    return _forward(dims, x, c, ctx, c_ctx, w_ada, b_ada, norm_pre, norm_post, w_in, w_out,
                    lam_q1, lam_k1, lam_q2, lam_k2, attn_subln, conv_w, conv_b, i_bias, f_bias, mlstm_norm)
```

```python
import functools
import math
from typing import NamedTuple

import numpy as np
import jax
import jax.numpy as jnp
from jax import lax
from jax.experimental import pallas as pl
from jax.experimental.pallas import tpu as pltpu

F32 = jnp.float32
BF16 = jnp.bfloat16

LANES = 128
ATT_V_DIM = 128
ATT_QK_DIM = 64
MLSTM_V_DIM = 256
MLSTM_QK_DIM = 128
CONV_WIDTH = 3
GATE_SOFTCAP = 15.0
ROPE_BASE = 10000.0
GRID_W = 64
EPS = 1e-6
MOD_ROWS = 16
VMEM_LIMIT = 56 * 1024 * 1024
NT = (((1,), (1,)), ((), ()))
MIN_SOFTMAX_DENOMINATOR = 2.0 ** -60


class Dims(NamedTuple):
    batch: int
    seq: int
    ctx: int
    d_model: int

    @property
    def att_width(self):
        return self.d_model // 2

    @property
    def att_heads(self):
        return self.att_width // ATT_V_DIM

    @property
    def ml_width(self):
        return self.d_model - self.att_width

    @property
    def ml_heads(self):
        return self.ml_width // MLSTM_V_DIM

    @property
    def n_lat(self):
        return self.batch * self.seq

    @property
    def n_ctx(self):
        return self.batch * self.ctx

    @property
    def col_qa(self):
        return 0

    @property
    def col_ka(self):
        return self.att_width

    @property
    def col_va(self):
        return 2 * self.att_width

    @property
    def col_ga(self):
        return 3 * self.att_width

    @property
    def col_qb(self):
        return 4 * self.att_width

    @property
    def col_kb(self):
        return self.col_qb + self.ml_heads * MLSTM_QK_DIM

    @property
    def col_vb(self):
        return self.col_kb + self.ml_heads * MLSTM_QK_DIM

    @property
    def col_ob(self):
        return self.col_vb + self.ml_width

    @property
    def col_gb(self):
        return self.col_ob + self.ml_width

    @property
    def main_cols(self):
        return self.col_gb + self.ml_width


def _params(*sem):
    return pltpu.CompilerParams(dimension_semantics=sem, vmem_limit_bytes=VMEM_LIMIT)


def _silu(x):
    return x * jax.nn.sigmoid(x)


def _wprep_kernel(w_ref, wg_ref, perm_ref, place_ref, o_ref, og_ref, *, qk_tiles, q_tiles, q_scale):
    j = pl.program_id(1)

    @pl.when(j < qk_tiles)
    def _():
        x = (w_ref[0] * jnp.where(j < q_tiles, q_scale, 1.0)).astype(BF16)
        for r in range(0, x.shape[0], LANES):
            o_ref[0, r:r + LANES, :] = jnp.dot(
                perm_ref[...], x[r:r + LANES, :], preferred_element_type=F32).astype(BF16)

    @pl.when(j >= qk_tiles)
    def _():
        o_ref[0] = w_ref[0].astype(BF16)

    @pl.when(j == 0)
    def _():
        og_ref[0] = jnp.dot(place_ref[...], wg_ref[0].astype(BF16), preferred_element_type=F32).astype(BF16)


def _rope_row_permutation():
    perm = np.zeros((LANES, LANES), np.float32)
    for c in range(2):
        for i in range(ATT_QK_DIM // 2):
            for par in range(2):
                perm[par * ATT_QK_DIM + c * (ATT_QK_DIM // 2) + i, c * ATT_QK_DIM + 2 * i + par] = 1.0
    return jnp.asarray(perm, BF16)


def _gate_row_placement(dims):
    mh = dims.ml_heads
    place = np.zeros((mh * LANES, 4 * mh), np.float32)
    for kind in range(4):
        for h in range(mh):
            place[h * LANES + kind, kind * mh + h] = 1.0
    return jnp.asarray(place, BF16)


def _wprep_call(dims, w_in):
    depth, d, n_all = w_in.shape
    mh = dims.ml_heads
    tn = min(dims.att_width, 1024)
    n_gate = n_all - dims.main_cols
    assert dims.att_width % tn == 0 and dims.main_cols % tn == 0
    assert dims.main_cols % n_gate == 0 and n_gate == 4 * mh and n_gate % 8 == 0
    kernel = functools.partial(_wprep_kernel, qk_tiles=2 * dims.att_width // tn, q_tiles=dims.att_width // tn,
                               q_scale=ATT_QK_DIM ** -0.5 * math.log2(math.e))
    w_t = jnp.swapaxes(w_in, 1, 2)
    return pl.pallas_call(
        kernel,
        grid=(depth, dims.main_cols // tn),
        in_specs=[
            pl.BlockSpec((1, tn, d), lambda l, j: (l, j, 0)),
            pl.BlockSpec((1, n_gate, d), lambda l, j: (l, dims.main_cols // n_gate, 0)),
            pl.BlockSpec((LANES, LANES), lambda l, j: (0, 0)),
            pl.BlockSpec((mh * LANES, n_gate), lambda l, j: (0, 0)),
        ],
        out_specs=[
            pl.BlockSpec((1, tn, d), lambda l, j: (l, j, 0)),
            pl.BlockSpec((1, mh * LANES, d), lambda l, j: (l, 0, 0)),
        ],
        out_shape=[
            jax.ShapeDtypeStruct((depth, dims.main_cols, d), BF16),
            jax.ShapeDtypeStruct((depth, mh * LANES, d), BF16),
        ],
        compiler_params=_params("arbitrary", "arbitrary"),
        name="w_prep",
    )(w_t, w_t, _rope_row_permutation(), _gate_row_placement(dims))


def _cast_kernel(w_ref, o_ref):
    o_ref[...] = w_ref[...].astype(BF16)


def _cast_call(w):
    depth, r, c = w.shape
    tr = min(r, 512)
    return pl.pallas_call(
        _cast_kernel,
        grid=(depth, r // tr),
        in_specs=[pl.BlockSpec((1, tr, c), lambda l, i: (l, i, 0))],
        out_specs=pl.BlockSpec((1, tr, c), lambda l, i: (l, i, 0)),
        out_shape=jax.ShapeDtypeStruct(w.shape, BF16),
        compiler_params=_params("arbitrary", "arbitrary"),
        name="w_cast",
    )(w)


def _ada_kernel(c_ref, w_ref, b_ref, o_ref):
    s = _silu(c_ref[...]).astype(BF16)
    w = w_ref[0].astype(BF16)
    o_ref[0] = jnp.dot(s, w, preferred_element_type=F32) + b_ref[0]


def _ada_call(c_all, w_ada, b_ada):
    depth, d, n = w_ada.shape
    tn = min(n, 1024)
    return pl.pallas_call(
        _ada_kernel,
        grid=(depth, n // tn),
        in_specs=[
            pl.BlockSpec((MOD_ROWS, d), lambda l, j: (0, 0)),
            pl.BlockSpec((1, d, tn), lambda l, j: (l, 0, j)),
            pl.BlockSpec((1, 1, tn), lambda l, j: (l, 0, j)),
        ],
        out_specs=pl.BlockSpec((1, MOD_ROWS, tn), lambda l, j: (l, 0, j)),
        out_shape=jax.ShapeDtypeStruct((depth, MOD_ROWS, n), F32),
        compiler_params=_params("arbitrary", "arbitrary"),
        name="ada",
    )(c_all, w_ada, b_ada.reshape(depth, 1, n))


def _inproj_kernel(x_ref, shift_ref, scale_ref, nw_ref, w_ref, wg_ref, cos_ref, sin_ref, p_ref, g_ref, h_scr,
                   *, tm, tn, rows_per_mod, fixed_mod_row, rope_tiles):
    i = pl.program_id(0)
    j = pl.program_id(1)
    rc = min(tm, 256)

    @pl.when(j == 0)
    def _():
        row = fixed_mod_row if fixed_mod_row is not None else (i * tm) // rows_per_mod
        shift = shift_ref[pl.ds(row, 1), :]
        scale1 = 1.0 + scale_ref[pl.ds(row, 1), :]
        nw = nw_ref[...]
        for r in range(0, tm, rc):
            xf = x_ref[r:r + rc, :]
            ms = jnp.mean(xf * xf, axis=-1, keepdims=True)
            y = xf * lax.rsqrt(ms + EPS) * nw
            h_scr[r:r + rc, :] = (y * scale1 + shift).astype(BF16)
        g_ref[...] = lax.dot_general(h_scr[...], wg_ref[...], NT, preferred_element_type=F32)

    def project(rope):
        for n in range(0, tn, 2 * LANES):
            acc = lax.dot_general(h_scr[...], w_ref[n:n + 2 * LANES, :], NT, preferred_element_type=F32)
            if rope:
                for hh in range(2):
                    a = acc[:, hh * LANES:(hh + 1) * LANES]
                    r = a * cos_ref[...] + pltpu.roll(a, LANES // 2, axis=1) * sin_ref[...]
                    p_ref[:, n + hh * LANES:n + (hh + 1) * LANES] = r.astype(BF16)
            else:
                p_ref[:, n:n + 2 * LANES] = acc.astype(BF16)

    if rope_tiles:
        pl.when(j < rope_tiles)(lambda: project(True))
        pl.when(j >= rope_tiles)(lambda: project(False))
    else:
        project(False)


def _inproj_call(dims, x_flat, mod, nw, w_main, layer, w_gate, cos_t, sin_t, is_ctx):
    m, d = x_flat.shape
    n_main = w_main.shape[1]
    n_gate = w_gate.shape[1]
    tm = min(m if is_ctx else dims.seq, 1024)
    tn = min(n_main, 2 * dims.att_width)
    assert (2 * dims.att_width) % tn == 0
    kernel = functools.partial(
        _inproj_kernel, tm=tm, tn=tn, rows_per_mod=dims.seq,
        fixed_mod_row=dims.batch if is_ctx else None,
        rope_tiles=0 if is_ctx else (2 * dims.att_width) // tn)
    tiles_per_seq = max(dims.seq // tm, 1)
    return pl.pallas_call(
        kernel,
        grid=(m // tm, n_main // tn),
        in_specs=[
            pl.BlockSpec((tm, d), lambda i, j: (i, 0)),
            pl.BlockSpec((MOD_ROWS, d), lambda i, j: (0, 0)),
            pl.BlockSpec((MOD_ROWS, d), lambda i, j: (0, 1)),
            pl.BlockSpec((1, d), lambda i, j: (0, 0)),
            pl.BlockSpec((None, tn, d), lambda i, j: (layer, j, 0)),
            pl.BlockSpec((None, n_gate, d), lambda i, j: (layer, 0, 0)),
            pl.BlockSpec((tm, LANES), lambda i, j: (i % tiles_per_seq, 0)),
            pl.BlockSpec((tm, LANES), lambda i, j: (i % tiles_per_seq, 0)),
        ],
        out_specs=[
            pl.BlockSpec((tm, tn), lambda i, j: (i, j)),
            pl.BlockSpec((tm, n_gate), lambda i, j: (i, 0)),
        ],
        out_shape=[
            jax.ShapeDtypeStruct((m, n_main), BF16),
            jax.ShapeDtypeStruct((m, n_gate), F32),
        ],
        scratch_shapes=[pltpu.VMEM((tm, d), BF16)],
        compiler_params=_params("arbitrary", "arbitrary"),
        name="in_proj_ctx" if is_ctx else "in_proj",
    )(x_flat, mod, mod, nw, w_main, w_gate, cos_t, sin_t)


def _lambda(lam_ref, lam_init):
    lp = lam_ref[...]
    return (jnp.exp(jnp.sum(lp[0:1] * lp[1:2], axis=1, keepdims=True))
            - jnp.exp(jnp.sum(lp[2:3] * lp[3:4], axis=1, keepdims=True)) + lam_init)


def _first_sub_head_lanes():
    lane = lax.broadcasted_iota(jnp.int32, (1, LANES), 1)
    return (lane % (LANES // 2)) < (LANES // 4)


def _attn_finish(o_t, lam_init, sub_ref, g):
    ms = jnp.mean(o_t * o_t, axis=0, keepdims=True)
    y = (o_t * lax.rsqrt(ms + EPS)).T
    y = y * sub_ref[...] * (1.0 - lam_init)
    return (y * _silu(g.astype(F32))).astype(BF16)


def _attn_lat_kernel(q_ref, kc_ref, vc_ref, kl_ref, vl_ref, g_ref, lam_ref, sub_ref, o_ref,
                     vt_scr, s_scr, redo_scr, *, lam_init, tq, kc):
    ctx, seq = kc_ref.shape[0], kl_ref.shape[0]
    n_heads = q_ref.shape[1] // LANES
    n_tiles = seq // tq
    n_kc = (ctx + seq) // kc
    head_cols = [slice(h * LANES, (h + 1) * LANES) for h in range(n_heads)]

    for h, cols in enumerate(head_cols):
        for r in range(0, ctx, kc):
            vt_scr[h, :, r:r + kc] = vc_ref[r:r + kc, cols].T
        for r in range(0, seq, kc):
            vt_scr[h, :, ctx + r:ctx + r + kc] = vl_ref[r:r + kc, cols].T

    lam = _lambda(lam_ref, lam_init)
    first = _first_sub_head_lanes()

    row8 = lax.broadcasted_iota(jnp.int32, (8, LANES), 0)
    pick = jnp.where(jnp.logical_or(jnp.logical_and(row8 == 0, first),
                                    jnp.logical_and(row8 == 1, jnp.logical_not(first))), 1.0, 0.0).astype(BF16)

    def sq_norms(x):
        return lax.dot_general(pick, x * x, NT, preferred_element_type=F32)

    k_max2 = [jnp.maximum(jnp.max(sq_norms(kc_ref[:, cols]), axis=1, keepdims=True),
                          jnp.max(sq_norms(kl_ref[:, cols]), axis=1, keepdims=True)) for cols in head_cols]

    def sub_head(q, sub):
        return jnp.where(first if sub == 0 else jnp.logical_not(first), q, jnp.zeros_like(q))

    def scores(h, qs, slot):
        s_c = lax.dot_general(kc_ref[:, head_cols[h]], qs, NT, preferred_element_type=F32)
        s_l = lax.dot_general(kl_ref[:, head_cols[h]], qs, NT, preferred_element_type=F32)
        s_scr[slot, 0:ctx] = s_c
        s_scr[slot, ctx:] = s_l
        return s_c, s_l

    def softmax_values(h, slot, m):
        den = jnp.zeros((1, tq), F32)
        o_t = jnp.zeros((ATT_V_DIM, tq), F32)
        for i in range(n_kc):
            p = jnp.exp2(s_scr[slot, i * kc:(i + 1) * kc, :] - m)
            den = den + jnp.sum(p, axis=0, keepdims=True)
            o_t = o_t + jnp.dot(vt_scr[h, :, i * kc:(i + 1) * kc], p.astype(BF16), preferred_element_type=F32)
        return o_t * (1.0 / den), den

    o_first = None
    den_min = None
    unit = 0
    for h, cols in enumerate(head_cols):
        for t in range(n_tiles):
            rows = slice(t * tq, (t + 1) * tq)
            q = q_ref[rows, cols]
            q_norm2 = sq_norms(q)
            for sub in range(2):
                bound = jnp.sqrt(q_norm2[sub:sub + 1, :] * k_max2[h][sub:sub + 1, :]) * 1.05
                scores(h, sub_head(q, sub), unit % 2)
                o_t, den = softmax_values(h, unit % 2, bound)
                den_min = den if den_min is None else jnp.minimum(den_min, den)
                unit += 1
                if sub == 0:
                    o_first = o_t
                else:
                    o_ref[rows, cols] = _attn_finish(o_first - lam * o_t, lam_init, sub_ref, g_ref[rows, cols])

    ok = jnp.min(den_min, axis=1, keepdims=True) >= MIN_SOFTMAX_DENOMINATOR
    redo_scr[0] = jnp.where(ok, 0, 1)[0, 0]

    @pl.when(redo_scr[0] != 0)
    def _():
        for h, cols in enumerate(head_cols):
            def exact_tile(t, carry, h=h, cols=cols):
                rows = pl.ds(pl.multiple_of(t * tq, tq), tq)
                q = q_ref[rows, cols]
                outs = []
                for sub in range(2):
                    s_c, s_l = scores(h, sub_head(q, sub), sub)
                    m = jnp.maximum(jnp.max(s_c, axis=0, keepdims=True), jnp.max(s_l, axis=0, keepdims=True))
                    outs.append(softmax_values(h, sub, m)[0])
                o_ref[rows, cols] = _attn_finish(outs[0] - lam * outs[1], lam_init, sub_ref, g_ref[rows, cols])
                return carry

            lax.fori_loop(0, n_tiles, exact_tile, 0)


def _attn_ctx_kernel(q_ref, kc_ref, vc_ref, g_ref, lam_ref, sub_ref, o_ref, *, lam_init):
    lam = _lambda(lam_ref, lam_init)
    first = _first_sub_head_lanes()
    for h in range(q_ref.shape[1] // LANES):
        cols = slice(h * LANES, (h + 1) * LANES)
        q = q_ref[:, cols]
        k = kc_ref[:, cols]
        vt = vc_ref[:, cols].T
        outs = []
        for sub in range(2):
            qs = jnp.where(first if sub == 0 else jnp.logical_not(first), q, jnp.zeros_like(q))
            s = lax.dot_general(k, qs, NT, preferred_element_type=F32)
            p = jnp.exp2(s - jnp.max(s, axis=0, keepdims=True))
            den = jnp.sum(p, axis=0, keepdims=True)
            outs.append(jnp.dot(vt, p.astype(BF16), preferred_element_type=F32) * (1.0 / den))
        o_ref[:, cols] = _attn_finish(outs[0] - lam * outs[1], lam_init, sub_ref, g_ref[:, cols])


def _attn_call(dims, p_lat, p_ctx, lam_rows, subln, layer_idx, is_ctx):
    lam_init = 0.8 - 0.6 * math.exp(-0.3 * layer_idx)
    ah = dims.att_heads
    seq, ctx = dims.seq, dims.ctx
    small = [pl.BlockSpec((4, ATT_QK_DIM), lambda b, h: (0, 0)),
             pl.BlockSpec((1, ATT_V_DIM), lambda b, h: (0, 0))]
    if is_ctx:
        aw = dims.att_width
        wide = lambda off: pl.BlockSpec((ctx, aw), lambda b: (b, off // aw))
        return pl.pallas_call(
            functools.partial(_attn_ctx_kernel, lam_init=lam_init),
            grid=(dims.batch,),
            in_specs=[wide(dims.col_qa), wide(dims.col_ka), wide(dims.col_va), wide(dims.col_ga),
                      pl.BlockSpec((4, ATT_QK_DIM), lambda b: (0, 0)),
                      pl.BlockSpec((1, ATT_V_DIM), lambda b: (0, 0))],
            out_specs=pl.BlockSpec((ctx, aw), lambda b: (b, 0)),
            out_shape=jax.ShapeDtypeStruct((dims.n_ctx, aw), BF16),
            compiler_params=_params("arbitrary"),
            name="attn_ctx",
        )(p_ctx, p_ctx, p_ctx, p_ctx, lam_rows, subln)
    tq = min(seq, 512)
    kc = min(ctx, 256)
    hg = 2 if ah % 2 == 0 else 1
    w = hg * LANES
    lat_g = lambda off: pl.BlockSpec((seq, w), lambda b, h: (b, off // w + h))
    cx_g = lambda off: pl.BlockSpec((ctx, w), lambda b, h: (b, off // w + h))
    return pl.pallas_call(
        functools.partial(_attn_lat_kernel, lam_init=lam_init, tq=tq, kc=kc),
        grid=(dims.batch, ah // hg),
        in_specs=[lat_g(dims.col_qa), cx_g(dims.col_ka), cx_g(dims.col_va), lat_g(dims.col_ka),
                  lat_g(dims.col_va), lat_g(dims.col_ga)] + small,
        out_specs=pl.BlockSpec((seq, w), lambda b, h: (b, h)),
        out_shape=jax.ShapeDtypeStruct((dims.n_lat, dims.att_width), BF16),
        scratch_shapes=[
            pltpu.VMEM((hg, ATT_V_DIM, ctx + seq), BF16),
            pltpu.VMEM((2, ctx + seq, tq), F32),
            pltpu.SMEM((1,), jnp.int32),
        ],
        compiler_params=_params("arbitrary", "arbitrary"),
        name="attn",
    )(p_lat, p_ctx, p_ctx, p_lat, p_lat, p_lat, lam_rows, subln)


def _split3(x):
    hi = x.astype(BF16)
    r1 = x - hi.astype(F32)
    mid = r1.astype(BF16)
    lo = (r1 - mid.astype(F32)).astype(BF16)
    return hi, mid, lo


def _conv_silu(src_ref, w, b, scale, dst_ref):
    x = src_ref[...].astype(F32)
    t = x.shape[0]
    row = lax.broadcasted_iota(jnp.int32, (t, 1), 0)
    prev = jnp.where(row == 0, 0.0, pltpu.roll(x, 1, axis=0))
    nxt = jnp.where(row == t - 1, 0.0, pltpu.roll(x, t - 1, axis=0))
    y = b + w[0:1] * prev
    y = y + w[1:2] * x
    y = y + w[2:3] * nxt
    y = _silu(y)
    if scale is not None:
        y = y * scale
    dst_ref[...] = y.astype(BF16)


def _mlstm_kernel(*refs, chunk, with_ctx_out):
    (ql_ref, kl_ref, vl_ref, gl_ref, ol_ref, sl_ref,
     qc_ref, kc_ref, vc_ref, gc_ref, oc_ref, sc_ref,
     cwq_ref, cwk_ref, cbq_ref, cbk_ref, ib_ref, fb_ref, hn_ref) = refs[:19]
    refs = refs[19:]
    if with_ctx_out:
        yl_ref, yc_ref = refs[:2]
        refs = refs[2:]
    else:
        yl_ref, yc_ref = refs[0], None
        refs = refs[1:]
    qs_scr, ks_scr, vt_scr, gr_scr, h_scr, u_scr = refs
    L = chunk
    ctx, seq = qc_ref.shape[0], ql_ref.shape[0]
    ncc, nc = ctx // L, (ctx + seq) // L
    hd = pl.program_id(1)

    k_scale = MLSTM_QK_DIM ** -0.5
    _conv_silu(qc_ref, cwq_ref[...], cbq_ref[...], None, qs_scr.at[0:ctx])
    _conv_silu(kc_ref, cwk_ref[...], cbk_ref[...], k_scale, ks_scr.at[0:ctx])
    _conv_silu(ql_ref, cwq_ref[...], cbq_ref[...], None, qs_scr.at[ctx:])
    _conv_silu(kl_ref, cwk_ref[...], cbk_ref[...], k_scale, ks_scr.at[ctx:])

    def src_rows(c):
        return (True, slice(c * L, (c + 1) * L)) if c < ncc else (False, slice((c - ncc) * L, (c - ncc + 1) * L))

    for c in range(nc):
        is_ctx, rows = src_rows(c)
        vt_scr[c] = (vc_ref if is_ctx else vl_ref)[rows, :].T
        gr_scr[c] = (gc_ref if is_ctx else gl_ref)[rows, :].T[0:8, :]

    row = lax.broadcasted_iota(jnp.int32, (1, 8, 1), 1)
    bias = jnp.where(row == 0, ib_ref[0, hd],
                     jnp.where(row == 1, ib_ref[1, hd],
                               jnp.where(row == 2, fb_ref[0, hd],
                                         jnp.where(row == 3, fb_ref[1, hd], 0.0))))
    capped = GATE_SOFTCAP * jnp.tanh((gr_scr[...] + bias) * (1.0 / GATE_SOFTCAP))
    lsig = jnp.minimum(capped, 0.0) - jnp.log1p(jnp.exp(-jnp.abs(capped)))
    lf = jnp.where((row >= 2) & (row < 4), lsig, 0.0).reshape(nc * 8, L)
    ui = lax.broadcasted_iota(jnp.int32, (L, L), 0)
    ti = lax.broadcasted_iota(jnp.int32, (L, L), 1)
    tri_t = (ui <= ti).astype(BF16)
    hi, mid, lo = _split3(lf)
    prefix = (jnp.dot(hi, tri_t, preferred_element_type=F32) + jnp.dot(mid, tri_t, preferred_element_type=F32)
              + jnp.dot(lo, tri_t, preferred_element_type=F32))
    tot = prefix[:, L - 1:L]
    suffix = tot - prefix + lf
    row2 = lax.broadcasted_iota(jnp.int32, (nc * 8, 1), 0) % 8
    gates = jnp.where(row2 < 2, capped.reshape(nc * 8, L), jnp.where(row2 == 2, prefix, suffix))

    mask_f = ui <= ti
    mask_b = ui >= ti
    zeros_pad = jnp.zeros((LANES - 8, L), F32)

    rows_of = []
    local = [[None] * nc for _ in range(2)]
    for c in range(nc):
        g8 = gates[c * 8:(c + 1) * 8, :]
        t8 = tot[c * 8:(c + 1) * 8, :]
        dcols = jnp.concatenate([g8 - pltpu.roll(g8, 6, axis=0), zeros_pad], axis=0).T
        q = qs_scr[c * L:(c + 1) * L, :]
        k = ks_scr[c * L:(c + 1) * L, :]
        vt = vt_scr[c]
        qk = lax.dot_general(k, q, NT, preferred_element_type=F32)
        per_dir = []
        for d in range(2):
            irow, brow, blast = g8[d:d + 1, :], g8[2 + d:3 + d, :], t8[2 + d:3 + d, :]
            per_dir.append((irow, brow, blast))
            logw = jnp.where(mask_f if d == 0 else mask_b, brow + dcols[:, d:d + 1], -jnp.inf)
            a = jnp.max(logw, axis=0, keepdims=True)
            s = qk * jnp.exp(logw - a)
            h_scr[d, c] = jnp.dot(vt, s.astype(BF16), preferred_element_type=F32)
            dsum = jnp.sum(s, axis=0, keepdims=True)
            grow = blast - brow + irow
            gmax = jnp.max(grow, axis=1, keepdims=True)
            wk = jnp.exp(grow - gmax)
            u_scr[d, c] = jnp.dot(vt * wk.astype(BF16), k, preferred_element_type=F32)
            un = jnp.dot(jnp.broadcast_to(wk, (8, L)).astype(BF16), k, preferred_element_type=F32)
            local[d][c] = (a, dsum, gmax, un)
        rows_of.append(per_dir)

    order = [list(range(nc)), list(range(ncc - 1, -1, -1)) + list(range(nc - 1, ncc - 1, -1))]
    for d in range(2):
        ct = jnp.zeros((MLSTM_V_DIM, MLSTM_QK_DIM), F32)
        n = jnp.zeros((8, MLSTM_QK_DIM), F32)
        m = jnp.zeros((1, 1), F32)
        for c in order[d]:
            irow, brow, blast = rows_of[c][d]
            a, dsum, gmax, un = local[d][c]
            q = qs_scr[c * L:(c + 1) * L, :]
            inter = brow + m
            m_t = jnp.maximum(inter, a)
            e_inter = jnp.exp(inter - m_t)
            e_intra = jnp.exp(a - m_t)
            qc = lax.dot_general(ct.astype(BF16), q, NT, preferred_element_type=F32)
            qn = lax.dot_general(n.astype(BF16), q, NT, preferred_element_type=F32)[0:1, :]
            numer = e_inter * qc + e_intra * h_scr[d, c]
            denom = e_inter * qn + e_intra * dsum
            h_scr[d, c] = numer * (1.0 / jnp.maximum(jnp.abs(denom), jnp.exp(-m_t)))
            m_new = jnp.maximum(blast + m, gmax)
            decay = jnp.exp(blast + m - m_new)
            gain = jnp.exp(gmax - m_new)
            ct = decay * ct + gain * u_scr[d, c]
            n = decay * n + gain * un
            m = m_new

    hn = hn_ref[...]
    for c in range(nc):
        is_ctx, rows = src_rows(c)
        if is_ctx and not with_ctx_out:
            continue
        o_ref, s_ref, y_ref = (oc_ref, sc_ref, yc_ref) if is_ctx else (ol_ref, sl_ref, yl_ref)
        hsum = h_scr[0, c] + h_scr[1, c]
        ms = jnp.mean(hsum * hsum, axis=0, keepdims=True)
        y = (hsum * lax.rsqrt(ms + EPS)).T * hn
        y = y * jax.nn.sigmoid(o_ref[rows, :].astype(F32))
        y_ref[rows, :] = (y * _silu(s_ref[rows, :].astype(F32))).astype(BF16)


def _mlstm_call(dims, p_lat, g_lat, p_ctx, g_ctx, conv_w, conv_b, i_bias, f_bias, head_norm, with_ctx_out):
    mh = dims.ml_heads
    seq, ctx = dims.seq, dims.ctx
    L = min(ctx, 256)
    qk, dv = MLSTM_QK_DIM, MLSTM_V_DIM
    lat = lambda width, off: pl.BlockSpec((seq, width), lambda b, h: (b, off // width + h))
    cx = lambda width, off: pl.BlockSpec((ctx, width), lambda b, h: (b, off // width + h))
    smem = pl.BlockSpec(memory_space=pltpu.SMEM)
    in_specs = [
        lat(qk, dims.col_qb), lat(qk, dims.col_kb), lat(dv, dims.col_vb), lat(LANES, 0),
        lat(dv, dims.col_ob), lat(dv, dims.col_gb),
        cx(qk, dims.col_qb), cx(qk, dims.col_kb), cx(dv, dims.col_vb), cx(LANES, 0),
        cx(dv, dims.col_ob), cx(dv, dims.col_gb),
        pl.BlockSpec((CONV_WIDTH, qk), lambda b, h: (0, h)),
        pl.BlockSpec((CONV_WIDTH, qk), lambda b, h: (0, mh + h)),
        pl.BlockSpec((1, qk), lambda b, h: (0, h)),
        pl.BlockSpec((1, qk), lambda b, h: (0, mh + h)),
        smem, smem,
        pl.BlockSpec((1, dv), lambda b, h: (0, h)),
    ]
    args = [p_lat, p_lat, p_lat, g_lat, p_lat, p_lat, p_ctx, p_ctx, p_ctx, g_ctx, p_ctx, p_ctx,
            conv_w, conv_w, conv_b, conv_b, i_bias, f_bias, head_norm]
    out_specs = [pl.BlockSpec((seq, dv), lambda b, h: (b, h))]
    out_shape = [jax.ShapeDtypeStruct((dims.n_lat, dims.ml_width), BF16)]
    if with_ctx_out:
        out_specs.append(pl.BlockSpec((ctx, dv), lambda b, h: (b, h)))
        out_shape.append(jax.ShapeDtypeStruct((dims.n_ctx, dims.ml_width), BF16))
    nc = (ctx + seq) // L
    scratch = [
        pltpu.VMEM((ctx + seq, qk), BF16), pltpu.VMEM((ctx + seq, qk), BF16),
        pltpu.VMEM((nc, dv, L), BF16),
        pltpu.VMEM((nc, 8, L), F32),
        pltpu.VMEM((2, nc, dv, L), F32),
        pltpu.VMEM((2, nc, dv, qk), F32),
    ]
    return pl.pallas_call(
        functools.partial(_mlstm_kernel, chunk=L, with_ctx_out=with_ctx_out),
        grid=(dims.batch, mh),
        in_specs=in_specs,
        out_specs=out_specs,
        out_shape=out_shape,
        scratch_shapes=scratch,
        compiler_params=_params("arbitrary", "arbitrary"),
        name="mlstm",
    )(*args)


def _outproj_kernel(ya_ref, ym_ref, w_ref, x_ref, gate_ref, nw_ref, o_ref, *, tm, rows_per_mod, fixed_mod_row):
    i = pl.program_id(0)
    d = o_ref.shape[1]
    ka = ya_ref.shape[1]
    nc = min(d, 512)
    ss = jnp.zeros((tm, 1), F32)
    for n in range(0, d, nc):
        acc = (jnp.dot(ya_ref[...], w_ref[0:ka, n:n + nc], preferred_element_type=F32)
               + jnp.dot(ym_ref[...], w_ref[ka:, n:n + nc], preferred_element_type=F32))
        ss = ss + jnp.sum(acc * acc, axis=1, keepdims=True)
        o_ref[:, n:n + nc] = acc
    row = fixed_mod_row if fixed_mod_row is not None else (i * tm) // rows_per_mod
    gate = gate_ref[pl.ds(row, 1), :]
    inv = lax.rsqrt(ss * (1.0 / d) + EPS)
    o_ref[...] = x_ref[...] + gate * (o_ref[...] * inv * nw_ref[...])


def _outproj_call(dims, y_att, y_ml, w_out, layer, x_flat, mod, nw, is_ctx):
    m, d = x_flat.shape
    tm = min(m, 512)
    kernel = functools.partial(_outproj_kernel, tm=tm, rows_per_mod=dims.seq,
                               fixed_mod_row=dims.batch if is_ctx else None)
    return pl.pallas_call(
        kernel,
        grid=(m // tm,),
        in_specs=[
            pl.BlockSpec((tm, y_att.shape[1]), lambda i: (i, 0)),
            pl.BlockSpec((tm, y_ml.shape[1]), lambda i: (i, 0)),
            pl.BlockSpec((None, d, d), lambda i: (layer, 0, 0)),
            pl.BlockSpec((tm, d), lambda i: (i, 0)),
            pl.BlockSpec((MOD_ROWS, d), lambda i: (0, 2)),
            pl.BlockSpec((1, d), lambda i: (0, 0)),
        ],
        out_specs=pl.BlockSpec((tm, d), lambda i: (i, 0)),
        out_shape=jax.ShapeDtypeStruct((m, d), F32),
        compiler_params=_params("arbitrary"),
        name="out_proj_ctx" if is_ctx else "out_proj",
    )(y_att, y_ml, w_out, x_flat, mod, nw)


def _rope_tables(seq):
    rows = seq // GRID_W
    row = jnp.repeat(jnp.arange(rows), GRID_W).astype(F32)
    col = jnp.tile(jnp.arange(GRID_W), rows).astype(F32)
    n_freq = ATT_QK_DIM // 4
    inv = ROPE_BASE ** (-jnp.arange(n_freq, dtype=F32) / n_freq)
    ang = jnp.concatenate([row[:, None] * inv, col[:, None] * inv], axis=-1)
    cos, sin = jnp.cos(ang), jnp.sin(ang)
    return jnp.tile(cos, (1, 4)), jnp.concatenate([-sin, -sin, sin, sin], axis=-1)


def _forward(dims, x, c, ctx, c_ctx, w_ada, b_ada, norm_pre, norm_post, w_in, w_out,
             lam_q1, lam_k1, lam_q2, lam_k2, attn_subln, conv_w, conv_b, i_bias, f_bias, mlstm_norm):
    depth = w_ada.shape[0]
    d = dims.d_model
    x_flat = x.reshape(dims.n_lat, d)
    xc_flat = ctx.reshape(dims.n_ctx, d)
    c_all = jnp.concatenate(
        [c, c_ctx[None], jnp.zeros((MOD_ROWS - dims.batch - 1, d), F32)], axis=0)
    mod_all = _ada_call(c_all, w_ada, b_ada)
    w_main, w_gate = _wprep_call(dims, w_in)
    w_out_b = _cast_call(w_out)
    cos_t, sin_t = _rope_tables(dims.seq)
    for l in range(depth):
        need_ctx = l < depth - 1
        mod = mod_all[l]
        nw_pre = norm_pre[l].reshape(1, d)
        p_lat, g_lat = _inproj_call(dims, x_flat, mod, nw_pre, w_main, l, w_gate, cos_t, sin_t, False)
        p_ctx, g_ctx = _inproj_call(dims, xc_flat, mod, nw_pre, w_main, l, w_gate, cos_t, sin_t, True)
        lam_rows = jnp.stack([lam_q1[l], lam_k1[l], lam_q2[l], lam_k2[l]])
        subln = attn_subln[l].reshape(1, ATT_V_DIM)
        ya_lat = _attn_call(dims, p_lat, p_ctx, lam_rows, subln, l, False)
        ym = _mlstm_call(dims, p_lat, g_lat, p_ctx, g_ctx, conv_w[l], conv_b[l].reshape(1, -1),
                         i_bias[l], f_bias[l], mlstm_norm[l].reshape(1, -1), need_ctx)
        nw_post = norm_post[l].reshape(1, d)
        x_new = _outproj_call(dims, ya_lat, ym[0], w_out_b, l, x_flat, mod, nw_post, False)
        if need_ctx:
            ya_ctx = _attn_call(dims, p_lat, p_ctx, lam_rows, subln, l, True)
            xc_flat = _outproj_call(dims, ya_ctx, ym[1], w_out_b, l, xc_flat, mod, nw_post, True)
        x_flat = x_new
    return x_flat.reshape(x.shape)


def kernel(x, c, ctx, c_ctx, w_ada, b_ada, norm_pre, norm_post, w_in, w_out, lam_q1, lam_k1, lam_q2, lam_k2,
           attn_subln, conv_w, conv_b, i_bias, f_bias, mlstm_norm):
    dims = Dims(batch=x.shape[0], seq=x.shape[1], ctx=ctx.shape[1], d_model=x.shape[2])
    return _forward(dims, x, c, ctx, c_ctx, w_ada, b_ada, norm_pre, norm_post, w_in, w_out,
                    lam_q1, lam_k1, lam_q2, lam_k2, attn_subln, conv_w, conv_b, i_bias, f_bias, mlstm_norm)
```

```python
import functools
import math
from typing import NamedTuple

import numpy as np
import jax
import jax.numpy as jnp
from jax import lax
from jax.experimental import pallas as pl
from jax.experimental.pallas import tpu as pltpu

F32 = jnp.float32
BF16 = jnp.bfloat16

LANES = 128
ATT_V_DIM = 128
ATT_QK_DIM = 64
MLSTM_V_DIM = 256
MLSTM_QK_DIM = 128
CONV_WIDTH = 3
GATE_SOFTCAP = 15.0
ROPE_BASE = 10000.0
GRID_W = 64
EPS = 1e-6
MOD_ROWS = 16
VMEM_LIMIT = 56 * 1024 * 1024
NT = (((1,), (1,)), ((), ()))
MIN_SOFTMAX_DENOMINATOR = 2.0 ** -60


class Dims(NamedTuple):
    batch: int
    seq: int
    ctx: int
    d_model: int

    @property
    def att_width(self):
        return self.d_model // 2

    @property
    def att_heads(self):
        return self.att_width // ATT_V_DIM

    @property
    def ml_width(self):
        return self.d_model - self.att_width

    @property
    def ml_heads(self):
        return self.ml_width // MLSTM_V_DIM

    @property
    def n_lat(self):
        return self.batch * self.seq

    @property
    def n_ctx(self):
        return self.batch * self.ctx

    @property
    def col_qa(self):
        return 0

    @property
    def col_ka(self):
        return self.att_width

    @property
    def col_va(self):
        return 2 * self.att_width

    @property
    def col_ga(self):
        return 3 * self.att_width

    @property
    def col_qb(self):
        return 4 * self.att_width

    @property
    def col_kb(self):
        return self.col_qb + self.ml_heads * MLSTM_QK_DIM

    @property
    def col_vb(self):
        return self.col_kb + self.ml_heads * MLSTM_QK_DIM

    @property
    def col_ob(self):
        return self.col_vb + self.ml_width

    @property
    def col_gb(self):
        return self.col_ob + self.ml_width

    @property
    def main_cols(self):
        return self.col_gb + self.ml_width


def _params(*sem):
    return pltpu.CompilerParams(dimension_semantics=sem, vmem_limit_bytes=VMEM_LIMIT)


def _silu(x):
    return x * jax.nn.sigmoid(x)


def _wprep_kernel(w_ref, wg_ref, perm_ref, place_ref, o_ref, og_ref, *, qk_tiles, q_tiles, q_scale):
    j = pl.program_id(1)

    @pl.when(j < qk_tiles)
    def _():
        x = (w_ref[0] * jnp.where(j < q_tiles, q_scale, 1.0)).astype(BF16)
        for r in range(0, x.shape[0], LANES):
            o_ref[0, r:r + LANES, :] = jnp.dot(
                perm_ref[...], x[r:r + LANES, :], preferred_element_type=F32).astype(BF16)

    @pl.when(j >= qk_tiles)
    def _():
        o_ref[0] = w_ref[0].astype(BF16)

    @pl.when(j == 0)
    def _():
        og_ref[0] = jnp.dot(place_ref[...], wg_ref[0].astype(BF16), preferred_element_type=F32).astype(BF16)


def _rope_row_permutation():
    perm = np.zeros((LANES, LANES), np.float32)
    for c in range(2):
        for i in range(ATT_QK_DIM // 2):
            for par in range(2):
                perm[par * ATT_QK_DIM + c * (ATT_QK_DIM // 2) + i, c * ATT_QK_DIM + 2 * i + par] = 1.0
    return jnp.asarray(perm, BF16)


def _gate_row_placement(dims):
    mh = dims.ml_heads
    place = np.zeros((mh * LANES, 4 * mh), np.float32)
    for kind in range(4):
        for h in range(mh):
            place[h * LANES + kind, kind * mh + h] = 1.0
    return jnp.asarray(place, BF16)


def _wprep_call(dims, w_in):
    depth, d, n_all = w_in.shape
    mh = dims.ml_heads
    tn = min(dims.att_width, 1024)
    n_gate = n_all - dims.main_cols
    assert dims.att_width % tn == 0 and dims.main_cols % tn == 0
    assert dims.main_cols % n_gate == 0 and n_gate == 4 * mh and n_gate % 8 == 0
    kernel = functools.partial(_wprep_kernel, qk_tiles=2 * dims.att_width // tn, q_tiles=dims.att_width // tn,
                               q_scale=ATT_QK_DIM ** -0.5 * math.log2(math.e))
    w_t = jnp.swapaxes(w_in, 1, 2)
    return pl.pallas_call(
        kernel,
        grid=(depth, dims.main_cols // tn),
        in_specs=[
            pl.BlockSpec((1, tn, d), lambda l, j: (l, j, 0)),
            pl.BlockSpec((1, n_gate, d), lambda l, j: (l, dims.main_cols // n_gate, 0)),
            pl.BlockSpec((LANES, LANES), lambda l, j: (0, 0)),
            pl.BlockSpec((mh * LANES, n_gate), lambda l, j: (0, 0)),
        ],
        out_specs=[
            pl.BlockSpec((1, tn, d), lambda l, j: (l, j, 0)),
            pl.BlockSpec((1, mh * LANES, d), lambda l, j: (l, 0, 0)),
        ],
        out_shape=[
            jax.ShapeDtypeStruct((depth, dims.main_cols, d), BF16),
            jax.ShapeDtypeStruct((depth, mh * LANES, d), BF16),
        ],
        compiler_params=_params("arbitrary", "arbitrary"),
        name="w_prep",
    )(w_t, w_t, _rope_row_permutation(), _gate_row_placement(dims))


def _cast_kernel(w_ref, o_ref):
    o_ref[...] = w_ref[...].astype(BF16)


def _cast_call(w):
    depth, r, c = w.shape
    tr = min(r, 512)
    return pl.pallas_call(
        _cast_kernel,
        grid=(depth, r // tr),
        in_specs=[pl.BlockSpec((1, tr, c), lambda l, i: (l, i, 0))],
        out_specs=pl.BlockSpec((1, tr, c), lambda l, i: (l, i, 0)),
        out_shape=jax.ShapeDtypeStruct(w.shape, BF16),
        compiler_params=_params("arbitrary", "arbitrary"),
        name="w_cast",
    )(w)


def _ada_kernel(c_ref, w_ref, b_ref, o_ref):
    s = _silu(c_ref[...]).astype(BF16)
    w = w_ref[0].astype(BF16)
    o_ref[0] = jnp.dot(s, w, preferred_element_type=F32) + b_ref[0]


def _ada_call(c_all, w_ada, b_ada):
    depth, d, n = w_ada.shape
    tn = min(n, 1024)
    return pl.pallas_call(
        _ada_kernel,
        grid=(depth, n // tn),
        in_specs=[
            pl.BlockSpec((MOD_ROWS, d), lambda l, j: (0, 0)),
            pl.BlockSpec((1, d, tn), lambda l, j: (l, 0, j)),
            pl.BlockSpec((1, 1, tn), lambda l, j: (l, 0, j)),
        ],
        out_specs=pl.BlockSpec((1, MOD_ROWS, tn), lambda l, j: (l, 0, j)),
        out_shape=jax.ShapeDtypeStruct((depth, MOD_ROWS, n), F32),
        compiler_params=_params("arbitrary", "arbitrary"),
        name="ada",
    )(c_all, w_ada, b_ada.reshape(depth, 1, n))


def _inproj_kernel(x_ref, shift_ref, scale_ref, nw_ref, w_ref, wg_ref, cos_ref, sin_ref, p_ref, g_ref, h_scr,
                   *, tm, tn, rows_per_mod, fixed_mod_row, rope_tiles):
    i = pl.program_id(0)
    j = pl.program_id(1)
    rc = min(tm, 256)

    @pl.when(j == 0)
    def _():
        row = fixed_mod_row if fixed_mod_row is not None else (i * tm) // rows_per_mod
        shift = shift_ref[pl.ds(row, 1), :]
        scale1 = 1.0 + scale_ref[pl.ds(row, 1), :]
        nw = nw_ref[...]
        for r in range(0, tm, rc):
            xf = x_ref[r:r + rc, :]
            ms = jnp.mean(xf * xf, axis=-1, keepdims=True)
            y = xf * lax.rsqrt(ms + EPS) * nw
            h_scr[r:r + rc, :] = (y * scale1 + shift).astype(BF16)
        g_ref[...] = lax.dot_general(h_scr[...], wg_ref[...], NT, preferred_element_type=F32)

    def project(rope):
        for n in range(0, tn, 2 * LANES):
            acc = lax.dot_general(h_scr[...], w_ref[n:n + 2 * LANES, :], NT, preferred_element_type=F32)
            if rope:
                for hh in range(2):
                    a = acc[:, hh * LANES:(hh + 1) * LANES]
                    r = a * cos_ref[...] + pltpu.roll(a, LANES // 2, axis=1) * sin_ref[...]
                    p_ref[:, n + hh * LANES:n + (hh + 1) * LANES] = r.astype(BF16)
            else:
                p_ref[:, n:n + 2 * LANES] = acc.astype(BF16)

    if rope_tiles:
        pl.when(j < rope_tiles)(lambda: project(True))
        pl.when(j >= rope_tiles)(lambda: project(False))
    else:
        project(False)


def _inproj_call(dims, x_flat, mod, nw, w_main, layer, w_gate, cos_t, sin_t, is_ctx):
    m, d = x_flat.shape
    n_main = w_main.shape[1]
    n_gate = w_gate.shape[1]
    tm = min(m if is_ctx else dims.seq, 1024)
    tn = min(n_main, 2 * dims.att_width)
    assert (2 * dims.att_width) % tn == 0
    kernel = functools.partial(
        _inproj_kernel, tm=tm, tn=tn, rows_per_mod=dims.seq,
        fixed_mod_row=dims.batch if is_ctx else None,
        rope_tiles=0 if is_ctx else (2 * dims.att_width) // tn)
    tiles_per_seq = max(dims.seq // tm, 1)
    return pl.pallas_call(
        kernel,
        grid=(m // tm, n_main // tn),
        in_specs=[
            pl.BlockSpec((tm, d), lambda i, j: (i, 0)),
            pl.BlockSpec((MOD_ROWS, d), lambda i, j: (0, 0)),
            pl.BlockSpec((MOD_ROWS, d), lambda i, j: (0, 1)),
            pl.BlockSpec((1, d), lambda i, j: (0, 0)),
            pl.BlockSpec((None, tn, d), lambda i, j: (layer, j, 0)),
            pl.BlockSpec((None, n_gate, d), lambda i, j: (layer, 0, 0)),
            pl.BlockSpec((tm, LANES), lambda i, j: (i % tiles_per_seq, 0)),
            pl.BlockSpec((tm, LANES), lambda i, j: (i % tiles_per_seq, 0)),
        ],
        out_specs=[
            pl.BlockSpec((tm, tn), lambda i, j: (i, j)),
            pl.BlockSpec((tm, n_gate), lambda i, j: (i, 0)),
        ],
        out_shape=[
            jax.ShapeDtypeStruct((m, n_main), BF16),
            jax.ShapeDtypeStruct((m, n_gate), F32),
        ],
        scratch_shapes=[pltpu.VMEM((tm, d), BF16)],
        compiler_params=_params("arbitrary", "arbitrary"),
        name="in_proj_ctx" if is_ctx else "in_proj",
    )(x_flat, mod, mod, nw, w_main, w_gate, cos_t, sin_t)


def _lambda(lam_ref, lam_init):
    lp = lam_ref[...]
    return (jnp.exp(jnp.sum(lp[0:1] * lp[1:2], axis=1, keepdims=True))
            - jnp.exp(jnp.sum(lp[2:3] * lp[3:4], axis=1, keepdims=True)) + lam_init)


def _first_sub_head_lanes():
    lane = lax.broadcasted_iota(jnp.int32, (1, LANES), 1)
    return (lane % (LANES // 2)) < (LANES // 4)


def _attn_finish(o_t, lam_init, sub_ref, g):
    ms = jnp.mean(o_t * o_t, axis=0, keepdims=True)
    y = (o_t * lax.rsqrt(ms + EPS)).T
    y = y * sub_ref[...] * (1.0 - lam_init)
    return (y * _silu(g.astype(F32))).astype(BF16)


def _attn_lat_kernel(q_ref, kc_ref, vc_ref, kl_ref, vl_ref, g_ref, lam_ref, sub_ref, o_ref,
                     vt_scr, s_scr, redo_scr, *, lam_init, tq, kc):
    ctx, seq = kc_ref.shape[0], kl_ref.shape[0]
    n_heads = q_ref.shape[1] // LANES
    n_tiles = seq // tq
    n_kc = (ctx + seq) // kc
    head_cols = [slice(h * LANES, (h + 1) * LANES) for h in range(n_heads)]

    for h, cols in enumerate(head_cols):
        for r in range(0, ctx, kc):
            vt_scr[h, :, r:r + kc] = vc_ref[r:r + kc, cols].T
        for r in range(0, seq, kc):
            vt_scr[h, :, ctx + r:ctx + r + kc] = vl_ref[r:r + kc, cols].T

    lam = _lambda(lam_ref, lam_init)
    first = _first_sub_head_lanes()

    row8 = lax.broadcasted_iota(jnp.int32, (8, LANES), 0)
    pick = jnp.where(jnp.logical_or(jnp.logical_and(row8 == 0, first),
                                    jnp.logical_and(row8 == 1, jnp.logical_not(first))), 1.0, 0.0).astype(BF16)

    def sq_norms(x):
        return lax.dot_general(pick, x * x, NT, preferred_element_type=F32)

    k_max2 = [jnp.maximum(jnp.max(sq_norms(kc_ref[:, cols]), axis=1, keepdims=True),
                          jnp.max(sq_norms(kl_ref[:, cols]), axis=1, keepdims=True)) for cols in head_cols]

    def sub_head(q, sub):
        return jnp.where(first if sub == 0 else jnp.logical_not(first), q, jnp.zeros_like(q))

    def scores(h, qs, slot):
        s_c = lax.dot_general(kc_ref[:, head_cols[h]], qs, NT, preferred_element_type=F32)
        s_l = lax.dot_general(kl_ref[:, head_cols[h]], qs, NT, preferred_element_type=F32)
        s_scr[slot, 0:ctx] = s_c
        s_scr[slot, ctx:] = s_l
        return s_c, s_l

    def softmax_values(h, slot, m):
        den = jnp.zeros((1, tq), F32)
        o_t = jnp.zeros((ATT_V_DIM, tq), F32)
        for i in range(n_kc):
            p = jnp.exp2(s_scr[slot, i * kc:(i + 1) * kc, :] - m)
            den = den + jnp.sum(p, axis=0, keepdims=True)
            o_t = o_t + jnp.dot(vt_scr[h, :, i * kc:(i + 1) * kc], p.astype(BF16), preferred_element_type=F32)
        return o_t * (1.0 / den), den

    o_first = None
    den_min = None
    unit = 0
    for h, cols in enumerate(head_cols):
        for t in range(n_tiles):
            rows = slice(t * tq, (t + 1) * tq)
            q = q_ref[rows, cols]
            q_norm2 = sq_norms(q)
            for sub in range(2):
                bound = jnp.sqrt(q_norm2[sub:sub + 1, :] * k_max2[h][sub:sub + 1, :]) * 1.05
                scores(h, sub_head(q, sub), unit % 2)
                o_t, den = softmax_values(h, unit % 2, bound)
                den_min = den if den_min is None else jnp.minimum(den_min, den)
                unit += 1
                if sub == 0:
                    o_first = o_t
                else:
                    o_ref[rows, cols] = _attn_finish(o_first - lam * o_t, lam_init, sub_ref, g_ref[rows, cols])

    ok = jnp.min(den_min, axis=1, keepdims=True) >= MIN_SOFTMAX_DENOMINATOR
    redo_scr[0] = jnp.where(ok, 0, 1)[0, 0]

    @pl.when(redo_scr[0] != 0)
    def _():
        for h, cols in enumerate(head_cols):
            def exact_tile(t, carry, h=h, cols=cols):
                rows = pl.ds(pl.multiple_of(t * tq, tq), tq)
                q = q_ref[rows, cols]
                outs = []
                for sub in range(2):
                    s_c, s_l = scores(h, sub_head(q, sub), sub)
                    m = jnp.maximum(jnp.max(s_c, axis=0, keepdims=True), jnp.max(s_l, axis=0, keepdims=True))
                    outs.append(softmax_values(h, sub, m)[0])
                o_ref[rows, cols] = _attn_finish(outs[0] - lam * outs[1], lam_init, sub_ref, g_ref[rows, cols])
                return carry

            lax.fori_loop(0, n_tiles, exact_tile, 0)


def _attn_ctx_kernel(q_ref, kc_ref, vc_ref, g_ref, lam_ref, sub_ref, o_ref, *, lam_init):
    lam = _lambda(lam_ref, lam_init)
    first = _first_sub_head_lanes()
    for h in range(q_ref.shape[1] // LANES):
        cols = slice(h * LANES, (h + 1) * LANES)
        q = q_ref[:, cols]
        k = kc_ref[:, cols]
        vt = vc_ref[:, cols].T
        outs = []
        for sub in range(2):
            qs = jnp.where(first if sub == 0 else jnp.logical_not(first), q, jnp.zeros_like(q))
            s = lax.dot_general(k, qs, NT, preferred_element_type=F32)
            p = jnp.exp2(s - jnp.max(s, axis=0, keepdims=True))
            den = jnp.sum(p, axis=0, keepdims=True)
            outs.append(jnp.dot(vt, p.astype(BF16), preferred_element_type=F32) * (1.0 / den))
        o_ref[:, cols] = _attn_finish(outs[0] - lam * outs[1], lam_init, sub_ref, g_ref[:, cols])


def _attn_call(dims, p_lat, p_ctx, lam_rows, subln, layer_idx, is_ctx):
    lam_init = 0.8 - 0.6 * math.exp(-0.3 * layer_idx)
    ah = dims.att_heads
    seq, ctx = dims.seq, dims.ctx
    small = [pl.BlockSpec((4, ATT_QK_DIM), lambda b, h: (0, 0)),
             pl.BlockSpec((1, ATT_V_DIM), lambda b, h: (0, 0))]
    if is_ctx:
        aw = dims.att_width
        wide = lambda off: pl.BlockSpec((ctx, aw), lambda b: (b, off // aw))
        return pl.pallas_call(
            functools.partial(_attn_ctx_kernel, lam_init=lam_init),
            grid=(dims.batch,),
            in_specs=[wide(dims.col_qa), wide(dims.col_ka), wide(dims.col_va), wide(dims.col_ga),
                      pl.BlockSpec((4, ATT_QK_DIM), lambda b: (0, 0)),
                      pl.BlockSpec((1, ATT_V_DIM), lambda b: (0, 0))],
            out_specs=pl.BlockSpec((ctx, aw), lambda b: (b, 0)),
            out_shape=jax.ShapeDtypeStruct((dims.n_ctx, aw), BF16),
            compiler_params=_params("arbitrary"),
            name="attn_ctx",
        )(p_ctx, p_ctx, p_ctx, p_ctx, lam_rows, subln)
    tq = min(seq, 512)
    kc = min(ctx, 256)
    hg = 2 if ah % 2 == 0 else 1
    w = hg * LANES
    lat_g = lambda off: pl.BlockSpec((seq, w), lambda b, h: (b, off // w + h))
    cx_g = lambda off: pl.BlockSpec((ctx, w), lambda b, h: (b, off // w + h))
    return pl.pallas_call(
        functools.partial(_attn_lat_kernel, lam_init=lam_init, tq=tq, kc=kc),
        grid=(dims.batch, ah // hg),
        in_specs=[lat_g(dims.col_qa), cx_g(dims.col_ka), cx_g(dims.col_va), lat_g(dims.col_ka),
                  lat_g(dims.col_va), lat_g(dims.col_ga)] + small,
        out_specs=pl.BlockSpec((seq, w), lambda b, h: (b, h)),
        out_shape=jax.ShapeDtypeStruct((dims.n_lat, dims.att_width), BF16),
        scratch_shapes=[
            pltpu.VMEM((hg, ATT_V_DIM, ctx + seq), BF16),
            pltpu.VMEM((2, ctx + seq, tq), F32),
            pltpu.SMEM((1,), jnp.int32),
        ],
        compiler_params=_params("arbitrary", "arbitrary"),
        name="attn",
    )(p_lat, p_ctx, p_ctx, p_lat, p_lat, p_lat, lam_rows, subln)


def _split3(x):
    hi = x.astype(BF16)
    r1 = x - hi.astype(F32)
    mid = r1.astype(BF16)
    lo = (r1 - mid.astype(F32)).astype(BF16)
    return hi, mid, lo


def _conv_silu(src_ref, w, b, scale, dst_ref):
    x = src_ref[...].astype(F32)
    t = x.shape[0]
    row = lax.broadcasted_iota(jnp.int32, (t, 1), 0)
    prev = jnp.where(row == 0, 0.0, pltpu.roll(x, 1, axis=0))
    nxt = jnp.where(row == t - 1, 0.0, pltpu.roll(x, t - 1, axis=0))
    y = b + w[0:1] * prev
    y = y + w[1:2] * x
    y = y + w[2:3] * nxt
    y = _silu(y)
    if scale is not None:
        y = y * scale
    dst_ref[...] = y.astype(BF16)


def _mlstm_kernel(*refs, chunk, with_ctx_out):
    (ql_ref, kl_ref, vl_ref, gl_ref, ol_ref, sl_ref,
     qc_ref, kc_ref, vc_ref, gc_ref, oc_ref, sc_ref,
     cwq_ref, cwk_ref, cbq_ref, cbk_ref, ib_ref, fb_ref, hn_ref) = refs[:19]
    refs = refs[19:]
    if with_ctx_out:
        yl_ref, yc_ref = refs[:2]
        refs = refs[2:]
    else:
        yl_ref, yc_ref = refs[0], None
        refs = refs[1:]
    qs_scr, ks_scr, vt_scr, gr_scr, h_scr, u_scr = refs
    L = chunk
    ctx, seq = qc_ref.shape[0], ql_ref.shape[0]
    ncc, nc = ctx // L, (ctx + seq) // L
    hd = pl.program_id(1)

    k_scale = MLSTM_QK_DIM ** -0.5
    _conv_silu(qc_ref, cwq_ref[...], cbq_ref[...], None, qs_scr.at[0:ctx])
    _conv_silu(kc_ref, cwk_ref[...], cbk_ref[...], k_scale, ks_scr.at[0:ctx])
    _conv_silu(ql_ref, cwq_ref[...], cbq_ref[...], None, qs_scr.at[ctx:])
    _conv_silu(kl_ref, cwk_ref[...], cbk_ref[...], k_scale, ks_scr.at[ctx:])

    def src_rows(c):
        return (True, slice(c * L, (c + 1) * L)) if c < ncc else (False, slice((c - ncc) * L, (c - ncc + 1) * L))

    for c in range(nc):
        is_ctx, rows = src_rows(c)
        vt_scr[c] = (vc_ref if is_ctx else vl_ref)[rows, :].T
        gr_scr[c] = (gc_ref if is_ctx else gl_ref)[rows, :].T[0:8, :]

    row = lax.broadcasted_iota(jnp.int32, (1, 8, 1), 1)
    bias = jnp.where(row == 0, ib_ref[0, hd],
                     jnp.where(row == 1, ib_ref[1, hd],
                               jnp.where(row == 2, fb_ref[0, hd],
                                         jnp.where(row == 3, fb_ref[1, hd], 0.0))))
    capped = GATE_SOFTCAP * jnp.tanh((gr_scr[...] + bias) * (1.0 / GATE_SOFTCAP))
    lsig = jnp.minimum(capped, 0.0) - jnp.log1p(jnp.exp(-jnp.abs(capped)))
    lf = jnp.where((row >= 2) & (row < 4), lsig, 0.0).reshape(nc * 8, L)
    ui = lax.broadcasted_iota(jnp.int32, (L, L), 0)
    ti = lax.broadcasted_iota(jnp.int32, (L, L), 1)
    tri_t = (ui <= ti).astype(BF16)
    hi, mid, lo = _split3(lf)
    prefix = (jnp.dot(hi, tri_t, preferred_element_type=F32) + jnp.dot(mid, tri_t, preferred_element_type=F32)
              + jnp.dot(lo, tri_t, preferred_element_type=F32))
    tot = prefix[:, L - 1:L]
    suffix = tot - prefix + lf
    row2 = lax.broadcasted_iota(jnp.int32, (nc * 8, 1), 0) % 8
    gates = jnp.where(row2 < 2, capped.reshape(nc * 8, L), jnp.where(row2 == 2, prefix, suffix))

    mask_f = ui <= ti
    mask_b = ui >= ti
    zeros_pad = jnp.zeros((LANES - 8, L), F32)

    rows_of = []
    local = [[None] * nc for _ in range(2)]
    for c in range(nc):
        g8 = gates[c * 8:(c + 1) * 8, :]
        t8 = tot[c * 8:(c + 1) * 8, :]
        dcols = jnp.concatenate([g8 - pltpu.roll(g8, 6, axis=0), zeros_pad], axis=0).T
        q = qs_scr[c * L:(c + 1) * L, :]
        k = ks_scr[c * L:(c + 1) * L, :]
        vt = vt_scr[c]
        qk = lax.dot_general(k, q, NT, preferred_element_type=F32)
        per_dir = []
        for d in range(2):
            irow, brow, blast = g8[d:d + 1, :], g8[2 + d:3 + d, :], t8[2 + d:3 + d, :]
            per_dir.append((irow, brow, blast))
            logw = jnp.where(mask_f if d == 0 else mask_b, brow + dcols[:, d:d + 1], -jnp.inf)
            a = jnp.max(logw, axis=0, keepdims=True)
            s = qk * jnp.exp(logw - a)
            h_scr[d, c] = jnp.dot(vt, s.astype(BF16), preferred_element_type=F32)
            dsum = jnp.sum(s, axis=0, keepdims=True)
            grow = blast - brow + irow
            gmax = jnp.max(grow, axis=1, keepdims=True)
            wk = jnp.exp(grow - gmax)
            u_scr[d, c] = jnp.dot(vt * wk.astype(BF16), k, preferred_element_type=F32)
            un = jnp.dot(jnp.broadcast_to(wk, (8, L)).astype(BF16), k, preferred_element_type=F32)
            local[d][c] = (a, dsum, gmax, un)
        rows_of.append(per_dir)

    order = [list(range(nc)), list(range(ncc - 1, -1, -1)) + list(range(nc - 1, ncc - 1, -1))]
    for d in range(2):
        ct = jnp.zeros((MLSTM_V_DIM, MLSTM_QK_DIM), F32)
        n = jnp.zeros((8, MLSTM_QK_DIM), F32)
        m = jnp.zeros((1, 1), F32)
        for c in order[d]:
            irow, brow, blast = rows_of[c][d]
            a, dsum, gmax, un = local[d][c]
            q = qs_scr[c * L:(c + 1) * L, :]
            inter = brow + m
            m_t = jnp.maximum(inter, a)
            e_inter = jnp.exp(inter - m_t)
            e_intra = jnp.exp(a - m_t)
            qc = lax.dot_general(ct.astype(BF16), q, NT, preferred_element_type=F32)
            qn = lax.dot_general(n.astype(BF16), q, NT, preferred_element_type=F32)[0:1, :]
            numer = e_inter * qc + e_intra * h_scr[d, c]
            denom = e_inter * qn + e_intra * dsum
            h_scr[d, c] = numer * (1.0 / jnp.maximum(jnp.abs(denom), jnp.exp(-m_t)))
            m_new = jnp.maximum(blast + m, gmax)
            decay = jnp.exp(blast + m - m_new)
            gain = jnp.exp(gmax - m_new)
            ct = decay * ct + gain * u_scr[d, c]
            n = decay * n + gain * un
            m = m_new

    hn = hn_ref[...]
    for c in range(nc):
        is_ctx, rows = src_rows(c)
        if is_ctx and not with_ctx_out:
            continue
        o_ref, s_ref, y_ref = (oc_ref, sc_ref, yc_ref) if is_ctx else (ol_ref, sl_ref, yl_ref)
        hsum = h_scr[0, c] + h_scr[1, c]
        ms = jnp.mean(hsum * hsum, axis=0, keepdims=True)
        y = (hsum * lax.rsqrt(ms + EPS)).T * hn
        y = y * jax.nn.sigmoid(o_ref[rows, :].astype(F32))
        y_ref[rows, :] = (y * _silu(s_ref[rows, :].astype(F32))).astype(BF16)


def _mlstm_call(dims, p_lat, g_lat, p_ctx, g_ctx, conv_w, conv_b, i_bias, f_bias, head_norm, with_ctx_out):
    mh = dims.ml_heads
    seq, ctx = dims.seq, dims.ctx
    L = min(ctx, 256)
    qk, dv = MLSTM_QK_DIM, MLSTM_V_DIM
    lat = lambda width, off: pl.BlockSpec((seq, width), lambda b, h: (b, off // width + h))
    cx = lambda width, off: pl.BlockSpec((ctx, width), lambda b, h: (b, off // width + h))
    smem = pl.BlockSpec(memory_space=pltpu.SMEM)
    in_specs = [
        lat(qk, dims.col_qb), lat(qk, dims.col_kb), lat(dv, dims.col_vb), lat(LANES, 0),
        lat(dv, dims.col_ob), lat(dv, dims.col_gb),
        cx(qk, dims.col_qb), cx(qk, dims.col_kb), cx(dv, dims.col_vb), cx(LANES, 0),
        cx(dv, dims.col_ob), cx(dv, dims.col_gb),
        pl.BlockSpec((CONV_WIDTH, qk), lambda b, h: (0, h)),
        pl.BlockSpec((CONV_WIDTH, qk), lambda b, h: (0, mh + h)),
        pl.BlockSpec((1, qk), lambda b, h: (0, h)),
        pl.BlockSpec((1, qk), lambda b, h: (0, mh + h)),
        smem, smem,
        pl.BlockSpec((1, dv), lambda b, h: (0, h)),
    ]
    args = [p_lat, p_lat, p_lat, g_lat, p_lat, p_lat, p_ctx, p_ctx, p_ctx, g_ctx, p_ctx, p_ctx,
            conv_w, conv_w, conv_b, conv_b, i_bias, f_bias, head_norm]
    out_specs = [pl.BlockSpec((seq, dv), lambda b, h: (b, h))]
    out_shape = [jax.ShapeDtypeStruct((dims.n_lat, dims.ml_width), BF16)]
    if with_ctx_out:
        out_specs.append(pl.BlockSpec((ctx, dv), lambda b, h: (b, h)))
        out_shape.append(jax.ShapeDtypeStruct((dims.n_ctx, dims.ml_width), BF16))
    nc = (ctx + seq) // L
    scratch = [
        pltpu.VMEM((ctx + seq, qk), BF16), pltpu.VMEM((ctx + seq, qk), BF16),
        pltpu.VMEM((nc, dv, L), BF16),
        pltpu.VMEM((nc, 8, L), F32),
        pltpu.VMEM((2, nc, dv, L), F32),
        pltpu.VMEM((2, nc, dv, qk), F32),
    ]
    return pl.pallas_call(
        functools.partial(_mlstm_kernel, chunk=L, with_ctx_out=with_ctx_out),
        grid=(dims.batch, mh),
        in_specs=in_specs,
        out_specs=out_specs,
        out_shape=out_shape,
        scratch_shapes=scratch,
        compiler_params=_params("arbitrary", "arbitrary"),
        name="mlstm",
    )(*args)


def _outproj_kernel(ya_ref, ym_ref, w_ref, x_ref, gate_ref, nw_ref, o_ref, *, tm, rows_per_mod, fixed_mod_row):
    i = pl.program_id(0)
    d = o_ref.shape[1]
    ka = ya_ref.shape[1]
    nc = min(d, 512)
    ss = jnp.zeros((tm, 1), F32)
    for n in range(0, d, nc):
        acc = (jnp.dot(ya_ref[...], w_ref[0:ka, n:n + nc], preferred_element_type=F32)
               + jnp.dot(ym_ref[...], w_ref[ka:, n:n + nc], preferred_element_type=F32))
        ss = ss + jnp.sum(acc * acc, axis=1, keepdims=True)
        o_ref[:, n:n + nc] = acc
    row = fixed_mod_row if fixed_mod_row is not None else (i * tm) // rows_per_mod
    gate = gate_ref[pl.ds(row, 1), :]
    inv = lax.rsqrt(ss * (1.0 / d) + EPS)
    o_ref[...] = x_ref[...] + gate * (o_ref[...] * inv * nw_ref[...])


def _outproj_call(dims, y_att, y_ml, w_out, layer, x_flat, mod, nw, is_ctx):
    m, d = x_flat.shape
    tm = min(m if is_ctx else dims.seq, 1024)
    kernel = functools.partial(_outproj_kernel, tm=tm, rows_per_mod=dims.seq,
                               fixed_mod_row=dims.batch if is_ctx else None)
    return pl.pallas_call(
        kernel,
        grid=(m // tm,),
        in_specs=[
            pl.BlockSpec((tm, y_att.shape[1]), lambda i: (i, 0)),
            pl.BlockSpec((tm, y_ml.shape[1]), lambda i: (i, 0)),
            pl.BlockSpec((None, d, d), lambda i: (layer, 0, 0), pipeline_mode=pl.Buffered(1)),
            pl.BlockSpec((tm, d), lambda i: (i, 0)),
            pl.BlockSpec((MOD_ROWS, d), lambda i: (0, 2)),
            pl.BlockSpec((1, d), lambda i: (0, 0)),
        ],
        out_specs=pl.BlockSpec((tm, d), lambda i: (i, 0)),
        out_shape=jax.ShapeDtypeStruct((m, d), F32),
        compiler_params=_params("arbitrary"),
        name="out_proj_ctx" if is_ctx else "out_proj",
    )(y_att, y_ml, w_out, x_flat, mod, nw)


def _rope_tables(seq):
    rows = seq // GRID_W
    row = jnp.repeat(jnp.arange(rows), GRID_W).astype(F32)
    col = jnp.tile(jnp.arange(GRID_W), rows).astype(F32)
    n_freq = ATT_QK_DIM // 4
    inv = ROPE_BASE ** (-jnp.arange(n_freq, dtype=F32) / n_freq)
    ang = jnp.concatenate([row[:, None] * inv, col[:, None] * inv], axis=-1)
    cos, sin = jnp.cos(ang), jnp.sin(ang)
    return jnp.tile(cos, (1, 4)), jnp.concatenate([-sin, -sin, sin, sin], axis=-1)


def _forward(dims, x, c, ctx, c_ctx, w_ada, b_ada, norm_pre, norm_post, w_in, w_out,
             lam_q1, lam_k1, lam_q2, lam_k2, attn_subln, conv_w, conv_b, i_bias, f_bias, mlstm_norm):
    depth = w_ada.shape[0]
    d = dims.d_model
    x_flat = x.reshape(dims.n_lat, d)
    xc_flat = ctx.reshape(dims.n_ctx, d)
    c_all = jnp.concatenate(
        [c, c_ctx[None], jnp.zeros((MOD_ROWS - dims.batch - 1, d), F32)], axis=0)
    mod_all = _ada_call(c_all, w_ada, b_ada)
    w_main, w_gate = _wprep_call(dims, w_in)
    w_out_b = _cast_call(w_out)
    cos_t, sin_t = _rope_tables(dims.seq)
    for l in range(depth):
        need_ctx = l < depth - 1
        mod = mod_all[l]
        nw_pre = norm_pre[l].reshape(1, d)
        p_lat, g_lat = _inproj_call(dims, x_flat, mod, nw_pre, w_main, l, w_gate, cos_t, sin_t, False)
        p_ctx, g_ctx = _inproj_call(dims, xc_flat, mod, nw_pre, w_main, l, w_gate, cos_t, sin_t, True)
        lam_rows = jnp.stack([lam_q1[l], lam_k1[l], lam_q2[l], lam_k2[l]])
        subln = attn_subln[l].reshape(1, ATT_V_DIM)
        ya_lat = _attn_call(dims, p_lat, p_ctx, lam_rows, subln, l, False)
        ym = _mlstm_call(dims, p_lat, g_lat, p_ctx, g_ctx, conv_w[l], conv_b[l].reshape(1, -1),
                         i_bias[l], f_bias[l], mlstm_norm[l].reshape(1, -1), need_ctx)
        nw_post = norm_post[l].reshape(1, d)
        x_new = _outproj_call(dims, ya_lat, ym[0], w_out_b, l, x_flat, mod, nw_post, False)
        if need_ctx:
            ya_ctx = _attn_call(dims, p_lat, p_ctx, lam_rows, subln, l, True)
            xc_flat = _outproj_call(dims, ya_ctx, ym[1], w_out_b, l, xc_flat, mod, nw_post, True)
        x_flat = x_new
    return x_flat.reshape(x.shape)


def kernel(x, c, ctx, c_ctx, w_ada, b_ada, norm_pre, norm_post, w_in, w_out, lam_q1, lam_k1, lam_q2, lam_k2,
           attn_subln, conv_w, conv_b, i_bias, f_bias, mlstm_norm):
    dims = Dims(batch=x.shape[0], seq=x.shape[1], ctx=ctx.shape[1], d_model=x.shape[2])
    return _forward(dims, x, c, ctx, c_ctx, w_ada, b_ada, norm_pre, norm_post, w_in, w_out,
                    lam_q1, lam_k1, lam_q2, lam_k2, attn_subln, conv_w, conv_b, i_bias, f_bias, mlstm_norm)
```
